```python
import math
import jax, jax.numpy as jnp
from jax import lax
import numpy as np


D_MODEL = 2048
BATCH = 8
SEQ = 4096
DEPTH = 4

N_MIXERS = 4
DEEPNORM_ALPHA = (2.0 * DEPTH) ** 0.25
DEEPNORM_BETA = (8.0 * DEPTH) ** -0.25
LN_EPS = 1e-5
RMS_EPS = 1e-6
N_MOD = 6

N_HGRN = (DEPTH + 3) // 4
N_SSD = (DEPTH + 2) // 4
N_DATT = (DEPTH + 1) // 4
N_RWKV = DEPTH // 4
N_DENSE = (DEPTH + 1) // 2
N_MOE = DEPTH // 2

HG_EXPAND = 128
HG_HEADS = D_MODEL // HG_EXPAND
HG_DK = HG_EXPAND
HG_DV = D_MODEL // HG_HEADS
HG_CHUNK = 32

SSD_DINNER = 2 * D_MODEL
SSD_HEADDIM = 64
SSD_HEADS = SSD_DINNER // SSD_HEADDIM
SSD_GROUPS = 8
SSD_DSTATE = 128
SSD_CONV = 4
SSD_CHUNK = 64
SSD_CONV_CH = SSD_DINNER + 2 * SSD_GROUPS * SSD_DSTATE
SSD_PROJ = SSD_DINNER + SSD_CONV_CH + SSD_HEADS

DA_PATTERNS = ((128, 1), (512, 4), (2048, 16))
DA_HEADS = 16
DA_HEADDIM = D_MODEL // DA_HEADS

RW_HEADDIM = 64
RW_HEADS = D_MODEL // RW_HEADDIM
RW_DECAY_LORA = 64
RW_AAA_LORA = 64
RW_GATE_LORA = 256
RW_GN_EPS = 64e-5

FFN_DENSE = 5504
MOE_EXPERTS = 8
MOE_TOPK = 2
MOE_FF = 4096

F32 = jnp.float32

kernel_name = 'hybrid_interleaved_deepnorm_adaln_block'


def layer_norm(x, w, b):
    xf = x.astype(F32)
    mu = jnp.mean(xf, axis=-1, keepdims=True)
    var = jnp.mean(jnp.square(xf - mu), axis=-1, keepdims=True)
    return ((xf - mu) * lax.rsqrt(var + LN_EPS) * w.astype(F32) + b.astype(F32)).astype(x.dtype)


def rms_norm(x, w):
    xf = x.astype(F32)
    return xf * lax.rsqrt(jnp.mean(jnp.square(xf), axis=-1, keepdims=True) + RMS_EPS) * w.astype(F32)


def gla_chunked(q, k, v, log_f, chunk):
    B_, S, H, dk = q.shape
    dv = v.shape[-1]
    nc = S // chunk

    def to_chunks(t):
        return t.reshape(B_, nc, chunk, H, t.shape[-1]).transpose(1, 0, 3, 2, 4)

    causal = jnp.tril(jnp.ones((chunk, chunk), dtype=bool))

    def step(state, inp):
        qc, kc, vc, gc = inp
        b = jnp.cumsum(gc, axis=2)
        rel = jnp.where(causal[:, :, None], b[:, :, :, None, :] - b[:, :, None, :, :], -jnp.inf)
        scores = jnp.einsum('bhtc,bhsc,bhtsc->bhts', qc, kc, jnp.exp(rel))
        o = jnp.einsum('bhts,bhsv->bhtv', scores, vc) + jnp.einsum('bhtc,bhcv->bhtv', qc * jnp.exp(b), state)
        b_end = b[:, :, -1:, :]
        state = jnp.exp(b_end[:, :, 0, :, None]) * state + jnp.einsum('bhsc,bhsv->bhcv', kc * jnp.exp(b_end - b), vc)
        return state, o

    s0 = jnp.zeros((B_, H, dk, dv), F32)
    _, o = lax.scan(step, s0, (to_chunks(q), to_chunks(k), to_chunks(v), to_chunks(log_f)))
    return o.transpose(1, 0, 3, 2, 4).reshape(B_, S, H, dv)


def hgrn2_mixer(h, w_in, lower_bound, norm_w, w_out):
    B_, S, _ = h.shape
    q, f, i, g = jnp.split(h @ w_in, 4, axis=-1)
    lb = lower_bound.astype(F32)
    f = lb + (1.0 - lb) * jax.nn.sigmoid(f.astype(F32))

    def heads(t):
        return t.astype(F32).reshape(B_, S, HG_HEADS, -1)

    o = gla_chunked(heads(jax.nn.silu(q)), heads(1.0 - f), heads(i), heads(jnp.log(f)), HG_CHUNK)
    o = rms_norm(o, norm_w) * heads(jax.nn.silu(g))
    return o.reshape(B_, S, D_MODEL).astype(h.dtype) @ w_out


def causal_depthwise_conv(x, w, b):
    K, ch = w.shape
    y = lax.conv_general_dilated(x, w[:, None, :].astype(x.dtype), window_strides=(1,),
                                 padding=((K - 1, 0),), dimension_numbers=('NWC', 'WIO', 'NWC'),
                                 feature_group_count=ch)
    return y + b.astype(x.dtype)


def ssd_chunked(x, dt, A, Bm, Cm, chunk):
    B_, S, H, P = x.shape
    G, N = Bm.shape[2], Bm.shape[3]
    hg = H // G
    nc = S // chunk
    xc = x.reshape(B_, nc, chunk, G, hg, P).transpose(1, 0, 2, 3, 4, 5)
    dtc = dt.reshape(B_, nc, chunk, G, hg).transpose(1, 0, 2, 3, 4)
    Bc = Bm.reshape(B_, nc, chunk, G, N).transpose(1, 0, 2, 3, 4)
    Cc = Cm.reshape(B_, nc, chunk, G, N).transpose(1, 0, 2, 3, 4)
    A_g = A.reshape(G, hg)
    causal = jnp.tril(jnp.ones((chunk, chunk), dtype=bool))

    def step(state, inp):
        xk, dtk, Bk, Ck = inp
        a = jnp.cumsum(dtk * A_g, axis=1)
        seg = jnp.where(causal[:, :, None, None], a[:, :, None] - a[:, None, :], -jnp.inf)
        cb = jnp.einsum('btgn,bsgn->btsg', Ck, Bk)
        y = jnp.einsum('btsg,btsgh,bsgh,bsghp->btghp', cb, jnp.exp(seg), dtk, xk)
        y = y + jnp.einsum('btgn,bghpn,btgh->btghp', Ck, state, jnp.exp(a))
        a_end = a[:, -1]
        wgt = jnp.exp(a_end[:, None] - a) * dtk
        state = jnp.exp(a_end)[..., None, None] * state + jnp.einsum('bsgh,bsghp,bsgn->bghpn', wgt, xk, Bk)
        return state, y

    s0 = jnp.zeros((B_, G, hg, P, N), F32)
    _, y = lax.scan(step, s0, (xc, dtc, Bc, Cc))
    return y.transpose(1, 0, 2, 3, 4, 5).reshape(B_, S, H, P)


def mamba2_mixer(h, w_in, conv_w, conv_b, dt_bias, a_log, d_skip, norm_w, w_out):
    B_, S, _ = h.shape
    proj = h @ w_in
    z = proj[..., :SSD_DINNER]
    xbc = proj[..., SSD_DINNER:SSD_DINNER + SSD_CONV_CH]
    dt = proj[..., SSD_DINNER + SSD_CONV_CH:]
    xbc = jax.nn.silu(causal_depthwise_conv(xbc, conv_w, conv_b)).astype(F32)
    gn = SSD_GROUPS * SSD_DSTATE
    xs = xbc[..., :SSD_DINNER].reshape(B_, S, SSD_HEADS, SSD_HEADDIM)
    bm = xbc[..., SSD_DINNER:SSD_DINNER + gn].reshape(B_, S, SSD_GROUPS, SSD_DSTATE)
    cm = xbc[..., SSD_DINNER + gn:].reshape(B_, S, SSD_GROUPS, SSD_DSTATE)
    dt = jax.nn.softplus(dt.astype(F32) + dt_bias.astype(F32))
    A = -jnp.exp(a_log.astype(F32))
    y = ssd_chunked(xs, dt, A, bm, cm, SSD_CHUNK) + d_skip.astype(F32)[:, None] * xs
    y = y.reshape(B_, S, SSD_DINNER) * jax.nn.silu(z.astype(F32))
    y = rms_norm(y.reshape(B_, S, SSD_GROUPS, -1), norm_w.reshape(SSD_GROUPS, -1))
    return y.reshape(B_, S, SSD_DINNER).astype(h.dtype) @ w_out


def alibi_slopes(n):
    return 2.0 ** (-8.0 * jnp.arange(1, n + 1, dtype=F32) / n)


def dilated_window_attention(q, k, v, slopes, window, dilation):
    B_, S, H, hd = q.shape
    span = window // dilation
    seg = span * dilation
    s_pad = -(-S // seg) * seg
    L = s_pad // dilation
    nb = L // span

    def to_sub(t):
        t = jnp.pad(t, ((0, 0), (0, s_pad - S), (0, 0), (0, 0)))
        return t.reshape(B_, L, dilation, H, hd).transpose(0, 2, 3, 1, 4)

    def banded(t):
        prev = jnp.pad(t, ((0, 0), (0, 0), (0, 0), (span, 0), (0, 0)))[:, :, :, :L]
        return jnp.concatenate([prev.reshape(B_, dilation, H, nb, span, hd),
                                t.reshape(B_, dilation, H, nb, span, hd)], axis=-2)

    qs = to_sub(q).reshape(B_, dilation, H, nb, span, hd)
    kb = banded(to_sub(k))
    vb = banded(to_sub(v))
    scores = jnp.einsum('brhnqd,brhnkd->brhnqk', qs, kb) * (hd ** -0.5)
    qi = jnp.arange(span)[:, None]
    ki = jnp.arange(2 * span)[None, :]
    dist = qi + span - ki
    blk = jnp.arange(nb)[:, None, None]
    valid = (dist >= 0) & (dist <= span) & (blk * span + ki - span >= 0)
    bias = -slopes[:, None, None, None] * (dist * dilation).astype(F32)
    scores = jnp.where(valid, scores + bias, -jnp.inf)
    m = jnp.max(scores, axis=-1, keepdims=True)
    p = jnp.exp(scores - m)
    den = jnp.sum(p, axis=-1, keepdims=True)
    o = jnp.einsum('brhnqk,brhnkd->brhnqd', p, vb) / den
    lse = (m + jnp.log(den))[..., 0]
    o = o.reshape(B_, dilation, H, L, hd).transpose(0, 3, 1, 2, 4).reshape(B_, s_pad, H, hd)[:, :S]
    lse = lse.reshape(B_, dilation, H, L).transpose(0, 3, 1, 2).reshape(B_, s_pad, H)[:, :S]
    return o, lse


def dilated_attention_mixer(h, w_in, w_out):
    B_, S, _ = h.shape
    n_grp = len(DA_PATTERNS)
    qkv = (h @ w_in).astype(F32).reshape(B_, S, n_grp, 3, DA_HEADS, DA_HEADDIM)
    slopes = alibi_slopes(DA_HEADS)
    outs, lses = [], []
    for g, (window, dilation) in enumerate(DA_PATTERNS):
        o, lse = dilated_window_attention(qkv[:, :, g, 0], qkv[:, :, g, 1], qkv[:, :, g, 2], slopes, window, dilation)
        outs.append(o)
        lses.append(lse)
    wts = jax.nn.softmax(jnp.stack(lses), axis=0)
    o = jnp.sum(wts[..., None] * jnp.stack(outs), axis=0)
    return o.reshape(B_, S, D_MODEL).astype(h.dtype) @ w_out


def rwkv7_scan(r, w, k, v, a, b):
    B_, S, H, dk = r.shape

    def step(state, inp):
        r_t, w_t, k_t, v_t, a_t, b_t = inp
        sa = jnp.einsum('bhvk,bhk->bhv', state, a_t)
        state = state * w_t[:, :, None, :] + sa[..., None] * b_t[:, :, None, :] + v_t[..., None] * k_t[:, :, None, :]
        return state, jnp.einsum('bhvk,bhk->bhv', state, r_t)

    s0 = jnp.zeros((B_, H, dk, dk), F32)
    _, y = lax.scan(step, s0, tuple(jnp.moveaxis(t, 1, 0) for t in (r, w, k, v, a, b)))
    return jnp.moveaxis(y, 0, 1)


def rwkv7_mixer(h, mu, w_rkv, w0, w1, w2, a0, a1, a2, g1, g2, k_k, k_a, r_k, ln_w, ln_b, w_out):
    B_, S, _ = h.shape
    xx = jnp.pad(h, ((0, 0), (1, 0), (0, 0)))[:, :-1] - h
    mixed = h[None] + xx[None] * mu[:, None, None, :]
    r, k, v = jnp.einsum('ibsd,ide->ibse', mixed[:3], w_rkv)
    xw, xa, xg = mixed[3], mixed[4], mixed[5]
    w_log = -jax.nn.softplus(-(w0 + jnp.tanh(xw @ w1) @ w2).astype(F32)) - 0.5
    decay = jnp.exp(-jnp.exp(w_log))
    a = jax.nn.sigmoid((a0 + (xa @ a1) @ a2).astype(F32))
    gate = jax.nn.sigmoid(xg @ g1) @ g2

    def heads(t):
        return t.astype(F32).reshape(B_, S, RW_HEADS, RW_HEADDIM)

    ah = heads(a)
    kk = heads(k * k_k)
    kk = kk * lax.rsqrt(jnp.maximum(jnp.sum(kk * kk, axis=-1, keepdims=True), 1e-24))
    kh = heads(k) * (1.0 + (ah - 1.0) * k_a.astype(F32).reshape(RW_HEADS, RW_HEADDIM))
    rh, vh = heads(r), heads(v)
    y = rwkv7_scan(rh, heads(decay), kh, vh, -kk, kk * ah)
    mu_y = jnp.mean(y, axis=-1, keepdims=True)
    var_y = jnp.mean(jnp.square(y - mu_y), axis=-1, keepdims=True)
    y = ((y - mu_y) * lax.rsqrt(var_y + RW_GN_EPS)).reshape(B_, S, D_MODEL) * ln_w.astype(F32) + ln_b.astype(F32)
    y = y + (jnp.sum(rh * kh * r_k.astype(F32), axis=-1, keepdims=True) * vh).reshape(B_, S, D_MODEL)
    return (y * gate.astype(F32)).astype(h.dtype) @ w_out


def swiglu(h, w_gu, w_down):
    gate, up = jnp.split(h @ w_gu, 2, axis=-1)
    return (jax.nn.silu(gate) * up) @ w_down


def moe_swiglu(h, w_router, w_gu, w_down):
    logits = (h @ w_router).astype(F32)
    top_v, top_i = lax.top_k(logits, MOE_TOPK)
    gates = jax.nn.softmax(top_v, axis=-1)
    combine = jnp.sum(jax.nn.one_hot(top_i, MOE_EXPERTS, dtype=F32) * gates[..., None], axis=-2).astype(h.dtype)
    out = jnp.zeros_like(h)
    for e in range(MOE_EXPERTS):
        out = out + combine[..., e:e + 1] * swiglu(h, w_gu[e], w_down[e])
    return out


def setup_inputs(seed: int = 0) -> dict:
    keys = iter(jax.random.split(jax.random.key(seed), 64))
    D = D_MODEL
    inv = D ** -0.5

    def nrm(shape, scale):
        return jax.random.normal(next(keys), shape, F32) * scale

    def unif(shape, lo, hi):
        return jax.random.uniform(next(keys), shape, F32, lo, hi)

    dt0 = jnp.exp(unif((N_SSD, SSD_HEADS), math.log(1e-3), math.log(1e-1)))
    return {
        'x': nrm((BATCH, SEQ, D), 1.0),
        'c': nrm((BATCH, D), 1.0),
        'ada_w': nrm((DEPTH, D, N_MOD * D), 0.1 * inv),
        'ada_b': nrm((DEPTH, N_MOD * D), 0.02),
        'ln_w': 1.0 + nrm((DEPTH, 2, D), 0.02),
        'ln_b': nrm((DEPTH, 2, D), 0.02),
        'hg_w_in': nrm((N_HGRN, D, 4 * D), inv),
        'hg_lb_logits': nrm((DEPTH + 1, D), 0.1),
        'hg_norm_w': 1.0 + nrm((N_HGRN, HG_DV), 0.02),
        'hg_w_out': nrm((N_HGRN, D, D), inv * DEEPNORM_BETA),
        'ssd_w_in': nrm((N_SSD, D, SSD_PROJ), inv),
        'ssd_conv_w': nrm((N_SSD, SSD_CONV, SSD_CONV_CH), SSD_CONV ** -0.5),
        'ssd_conv_b': nrm((N_SSD, SSD_CONV_CH), 0.02),
        'ssd_dt_bias': dt0 + jnp.log(-jnp.expm1(-dt0)),
        'ssd_a_log': jnp.log(unif((N_SSD, SSD_HEADS), 1.0, 16.0)),
        'ssd_d': 1.0 + nrm((N_SSD, SSD_HEADS), 0.02),
        'ssd_norm_w': 1.0 + nrm((N_SSD, SSD_DINNER), 0.02),
        'ssd_w_out': nrm((N_SSD, SSD_DINNER, D), SSD_DINNER ** -0.5 * DEEPNORM_BETA),
        'da_w_in': nrm((N_DATT, D, len(DA_PATTERNS) * 3 * D), inv),
        'da_w_out': nrm((N_DATT, D, D), inv * DEEPNORM_BETA),
        'rw_mu': unif((N_RWKV, 6, D), 0.0, 1.0),
        'rw_w_rkv': nrm((N_RWKV, 3, D, D), inv),
        'rw_w0': unif((N_RWKV, D), -6.0, 2.0),
        'rw_w1': nrm((N_RWKV, D, RW_DECAY_LORA), inv),
        'rw_w2': nrm((N_RWKV, RW_DECAY_LORA, D), 0.1 * RW_DECAY_LORA ** -0.5),
        'rw_a0': nrm((N_RWKV, D), 0.1),
        'rw_a1': nrm((N_RWKV, D, RW_AAA_LORA), inv),
        'rw_a2': nrm((N_RWKV, RW_AAA_LORA, D), 0.1 * RW_AAA_LORA ** -0.5),
        'rw_g1': nrm((N_RWKV, D, RW_GATE_LORA), inv),
        'rw_g2': nrm((N_RWKV, RW_GATE_LORA, D), RW_GATE_LORA ** -0.5),
        'rw_k_k': 0.85 + nrm((N_RWKV, D), 0.02),
        'rw_k_a': 1.0 + nrm((N_RWKV, D), 0.02),
        'rw_r_k': nrm((N_RWKV, RW_HEADS, RW_HEADDIM), 0.1),
        'rw_ln_w': 1.0 + nrm((N_RWKV, D), 0.02),
        'rw_ln_b': nrm((N_RWKV, D), 0.02),
        'rw_w_out': nrm((N_RWKV, D, D), inv * DEEPNORM_BETA),
        'ffn_w_gu': nrm((N_DENSE, D, 2 * FFN_DENSE), inv),
        'ffn_w_down': nrm((N_DENSE, FFN_DENSE, D), FFN_DENSE ** -0.5 * DEEPNORM_BETA),
        'moe_router': nrm((N_MOE, D, MOE_EXPERTS), inv),
        'moe_w_gu': nrm((N_MOE, MOE_EXPERTS, D, 2 * MOE_FF), inv),
        'moe_w_down': nrm((N_MOE, MOE_EXPERTS, MOE_FF, D), MOE_FF ** -0.5 * DEEPNORM_BETA),
    }


def reference(x, c, ada_w, ada_b, ln_w, ln_b,
              hg_w_in, hg_lb_logits, hg_norm_w, hg_w_out,
              ssd_w_in, ssd_conv_w, ssd_conv_b, ssd_dt_bias, ssd_a_log, ssd_d, ssd_norm_w, ssd_w_out,
              da_w_in, da_w_out,
              rw_mu, rw_w_rkv, rw_w0, rw_w1, rw_w2, rw_a0, rw_a1, rw_a2, rw_g1, rw_g2,
              rw_k_k, rw_k_a, rw_r_k, rw_ln_w, rw_ln_b, rw_w_out,
              ffn_w_gu, ffn_w_down, moe_router, moe_w_gu, moe_w_down):
    lower_bounds = jnp.cumsum(jax.nn.softmax(hg_lb_logits.astype(F32), axis=0), axis=0)
    for i in range(DEPTH):
        mixer, j = i % N_MIXERS, i // N_MIXERS
        mod = (c @ ada_w[i] + ada_b[i])[:, None, :]
        shift1, scale1, gate1, shift2, scale2, gate2 = jnp.split(mod, N_MOD, axis=-1)

        hin = x * (1.0 + scale1) + shift1
        if mixer == 0:
            y = hgrn2_mixer(hin, hg_w_in[j], lower_bounds[i], hg_norm_w[j], hg_w_out[j])
        elif mixer == 1:
            y = mamba2_mixer(hin, ssd_w_in[j], ssd_conv_w[j], ssd_conv_b[j], ssd_dt_bias[j],
                             ssd_a_log[j], ssd_d[j], ssd_norm_w[j], ssd_w_out[j])
        elif mixer == 2:
            y = dilated_attention_mixer(hin, da_w_in[j], da_w_out[j])
        else:
            y = rwkv7_mixer(hin, rw_mu[j], rw_w_rkv[j], rw_w0[j], rw_w1[j], rw_w2[j], rw_a0[j], rw_a1[j],
                            rw_a2[j], rw_g1[j], rw_g2[j], rw_k_k[j], rw_k_a[j], rw_r_k[j],
                            rw_ln_w[j], rw_ln_b[j], rw_w_out[j])
        x = layer_norm(DEEPNORM_ALPHA * x + (1.0 + gate1) * y, ln_w[i, 0], ln_b[i, 0])

        hin = x * (1.0 + scale2) + shift2
        if i % 2 == 0:
            y = swiglu(hin, ffn_w_gu[i // 2], ffn_w_down[i // 2])
        else:
            y = moe_swiglu(hin, moe_router[i // 2], moe_w_gu[i // 2], moe_w_down[i // 2])
        x = layer_norm(DEEPNORM_ALPHA * x + (1.0 + gate2) * y, ln_w[i, 1], ln_b[i, 1])
    return x
```

```python
import functools
import math

import numpy as np
import jax
import jax.numpy as jnp
from jax import lax
from jax.experimental import pallas as pl
from jax.experimental.pallas import tpu as pltpu

F32 = jnp.float32
BF16 = jnp.bfloat16

N_MOD = 6
LN_EPS = 1e-5
RMS_EPS = 1e-6

HG_DK = 128
GLA_CHUNK = 64
SSD_HEADDIM = 64
SSD_DSTATE = 128
SSD_CONV = 4
SSD_CHUNK = 256
DA_PATTERNS = ((128, 1), (512, 4), (2048, 16))
DA_HEADDIM = 128
DA_SPAN = 128
RW_HEADDIM = 64
RW_GN_EPS = 64e-5
MOE_TOPK = 2

LANE = 128
VMEM_LIMIT = 56 * 1024 * 1024


def _pick(n, cands):
    for c in cands:
        if n % c == 0:
            return c
    return n


def _cparams(sem):
    return pltpu.CompilerParams(dimension_semantics=sem, vmem_limit_bytes=VMEM_LIMIT)


def _sigmoid(x):
    return 1.0 / (1.0 + jnp.exp(-x))


def _silu(x):
    return x * _sigmoid(x)


def _softplus(x):
    return jnp.maximum(x, 0.0) + jnp.log(1.0 + jnp.exp(-jnp.abs(x)))


def _dot(a, b):
    return jnp.dot(a, b, preferred_element_type=F32)


def _dot_nt(a, b):
    return lax.dot_general(a, b, (((1,), (1,)), ((), ())), preferred_element_type=F32)


def _dot_tn(a, b):
    return lax.dot_general(a, b, (((0,), (0,)), ((), ())), preferred_element_type=F32)


def _ada_kernel(c_ref, w_ref, b_ref, o_ref):
    o_ref[0] = _dot(c_ref[...].astype(BF16), w_ref[0].astype(BF16)) + b_ref[0]


def ada_modulation(c, ada_w, ada_b):
    depth, d, n = ada_w.shape
    bsz = c.shape[0]
    tn = _pick(n, (1024, 512, 256, 128))
    return pl.pallas_call(
        _ada_kernel,
        grid=(depth, n // tn),
        in_specs=[
            pl.BlockSpec((bsz, d), lambda i, j: (0, 0)),
            pl.BlockSpec((1, d, tn), lambda i, j: (i, 0, j)),
            pl.BlockSpec((1, 1, tn), lambda i, j: (i, 0, j)),
        ],
        out_specs=pl.BlockSpec((1, bsz, tn), lambda i, j: (i, 0, j)),
        out_shape=jax.ShapeDtypeStruct((depth, bsz, n), F32),
        compiler_params=_cparams(("parallel", "parallel")),
        name="ada_modulation",
    )(c, ada_w, ada_b.reshape(depth, 1, n))


def _mm_mod_kernel(x_ref, mod_ref, w_ref, o_ref, h_scr, *, shift_row):
    @pl.when(pl.program_id(2) == 0)
    def _():
        shift = mod_ref[0, shift_row:shift_row + 1, :]
        scale = mod_ref[0, shift_row + 1:shift_row + 2, :]
        h_scr[...] = (x_ref[0] * (1.0 + scale) + shift).astype(BF16)

    o_ref[0] = _dot(h_scr[...], w_ref[...]).astype(o_ref.dtype)


def mm_mod(x, mod, w, shift_row, out_dtype=F32):
    bsz, s, d = x.shape
    n = w.shape[1]
    tm = _pick(s, (1024, 512, 256, 128))
    tn = _pick(n, (512, 256, 128))
    return pl.pallas_call(
        functools.partial(_mm_mod_kernel, shift_row=shift_row),
        grid=(bsz, s // tm, n // tn),
        in_specs=[
            pl.BlockSpec((1, tm, d), lambda b, i, j: (b, i, 0)),
            pl.BlockSpec((1, N_MOD, d), lambda b, i, j: (b, 0, 0)),
            pl.BlockSpec((d, tn), lambda b, i, j: (0, j)),
        ],
        out_specs=pl.BlockSpec((1, tm, tn), lambda b, i, j: (b, i, j)),
        out_shape=jax.ShapeDtypeStruct((bsz, s, n), out_dtype),
        scratch_shapes=[pltpu.VMEM((tm, d), BF16)],
        compiler_params=_cparams(("parallel", "parallel", "arbitrary")),
        name="mm_mod",
    )(x, mod, w)


def _mm_res_ln_kernel(*refs, gate_row, has_h2, nk, alpha):
    if has_h2:
        h_ref, h2_ref, w_ref, x_ref, mod_ref, lnw_ref, lnb_ref, o_ref, acc_ref = refs
        h = (h_ref[0] * h2_ref[0]).astype(BF16)
    else:
        h_ref, w_ref, x_ref, mod_ref, lnw_ref, lnb_ref, o_ref, acc_ref = refs
        h = h_ref[0].astype(BF16)
    k = pl.program_id(2)
    part = _dot(h, w_ref[...])

    @pl.when(k == 0)
    def _():
        acc_ref[...] = part

    @pl.when(k > 0)
    def _():
        acc_ref[...] += part

    @pl.when(k == nk - 1)
    def _():
        gate = mod_ref[0, gate_row:gate_row + 1, :]
        z = alpha * x_ref[0] + (1.0 + gate) * acc_ref[...]
        mu = jnp.mean(z, axis=-1, keepdims=True)
        zc = z - mu
        var = jnp.mean(zc * zc, axis=-1, keepdims=True)
        o_ref[0] = zc * lax.rsqrt(var + LN_EPS) * lnw_ref[...] + lnb_ref[...]


def mm_res_ln(h, w, x, mod, gate_row, ln_w, ln_b, alpha, h2=None):
    bsz, s, d = x.shape
    kdim = h.shape[-1]
    tm = _pick(s, (512, 256, 128))
    tk = _pick(kdim, (512, 256, 128))
    nk = kdim // tk
    h_spec = pl.BlockSpec((1, tm, tk), lambda b, i, k: (b, i, k))
    ins = [h] + ([h2] if h2 is not None else [])
    in_specs = [h_spec] * len(ins) + [
        pl.BlockSpec((tk, d), lambda b, i, k: (k, 0)),
        pl.BlockSpec((1, tm, d), lambda b, i, k: (b, i, 0)),
        pl.BlockSpec((1, N_MOD, d), lambda b, i, k: (b, 0, 0)),
        pl.BlockSpec((1, d), lambda b, i, k: (0, 0)),
        pl.BlockSpec((1, d), lambda b, i, k: (0, 0)),
    ]
    return pl.pallas_call(
        functools.partial(_mm_res_ln_kernel, gate_row=gate_row, has_h2=h2 is not None,
                          nk=nk, alpha=alpha),
        grid=(bsz, s // tm, nk),
        in_specs=in_specs,
        out_specs=pl.BlockSpec((1, tm, d), lambda b, i, k: (b, i, 0)),
        out_shape=jax.ShapeDtypeStruct((bsz, s, d), F32),
        scratch_shapes=[pltpu.VMEM((tm, d), F32)],
        compiler_params=_cparams(("parallel", "parallel", "arbitrary")),
        name="mm_res_ln",
    )(*ins, w, x, mod, ln_w.reshape(1, d), ln_b.reshape(1, d))


def _mm_swiglu_kernel(x_ref, mod_ref, wg_ref, wu_ref, o_ref, h_scr, *, shift_row):
    @pl.when(pl.program_id(2) == 0)
    def _():
        shift = mod_ref[0, shift_row:shift_row + 1, :]
        scale = mod_ref[0, shift_row + 1:shift_row + 2, :]
        h_scr[...] = (x_ref[0] * (1.0 + scale) + shift).astype(BF16)

    h = h_scr[...]
    g = _dot(h, wg_ref[...])
    u = _dot(h, wu_ref[...])
    o_ref[0] = (_silu(g) * u).astype(o_ref.dtype)


def mm_swiglu(x, mod, wg, wu, shift_row):
    bsz, s, d = x.shape
    f = wg.shape[1]
    tm = _pick(s, (1024, 512, 256, 128))
    tn = _pick(f, (512, 256, 128))
    return pl.pallas_call(
        functools.partial(_mm_swiglu_kernel, shift_row=shift_row),
        grid=(bsz, s // tm, f // tn),
        in_specs=[
            pl.BlockSpec((1, tm, d), lambda b, i, j: (b, i, 0)),
            pl.BlockSpec((1, N_MOD, d), lambda b, i, j: (b, 0, 0)),
            pl.BlockSpec((d, tn), lambda b, i, j: (0, j)),
            pl.BlockSpec((d, tn), lambda b, i, j: (0, j)),
        ],
        out_specs=pl.BlockSpec((1, tm, tn), lambda b, i, j: (b, i, j)),
        out_shape=jax.ShapeDtypeStruct((bsz, s, f), BF16),
        scratch_shapes=[pltpu.VMEM((tm, d), BF16)],
        compiler_params=_cparams(("parallel", "parallel", "arbitrary")),
        name="mm_swiglu",
    )(x, mod, wg, wu)


def _pad_cols(w, mult):
    n = w.shape[-1]
    pad = (-n) % mult
    return jnp.pad(w, ((0, 0), (0, pad))) if pad else w


def dense_ffn(x, mod, w_gu, w_down, ln_w, ln_b, alpha):
    f = w_gu.shape[1] // 2
    fmult = 512 if f >= 512 else LANE
    wg = _pad_cols(w_gu[:, :f], fmult).astype(BF16)
    wu = _pad_cols(w_gu[:, f:], fmult).astype(BF16)
    wd = jnp.pad(w_down, ((0, wg.shape[1] - f), (0, 0))).astype(BF16)
    hmid = mm_swiglu(x, mod, wg, wu, shift_row=3)
    return mm_res_ln(hmid, wd, x, mod, 5, ln_w, ln_b, alpha)


def _router_kernel(x_ref, mod_ref, wr_ref, o_ref, *, n_experts):
    shift = mod_ref[0, 3:4, :]
    scale = mod_ref[0, 4:5, :]
    h = x_ref[0] * (1.0 + scale) + shift
    logits = jnp.dot(h, wr_ref[...], preferred_element_type=F32, precision=lax.Precision.HIGHEST)
    lane = lax.broadcasted_iota(jnp.int32, logits.shape, 1)
    ninf = -jnp.inf
    lg = jnp.where(lane < n_experts, logits, ninf)
    m1 = jnp.max(lg, axis=-1, keepdims=True)
    i1 = jnp.min(jnp.where(lg == m1, lane, LANE), axis=-1, keepdims=True)
    lg2 = jnp.where(lane == i1, ninf, lg)
    m2 = jnp.max(lg2, axis=-1, keepdims=True)
    i2 = jnp.min(jnp.where(lg2 == m2, lane, LANE), axis=-1, keepdims=True)
    e2 = jnp.exp(m2 - m1)
    g1 = 1.0 / (1.0 + e2)
    o_ref[0] = jnp.where(lane == i1, g1, 0.0) + jnp.where(lane == i2, e2 * g1, 0.0)


def moe_router(x, mod, w_router):
    bsz, s, d = x.shape
    n_experts = w_router.shape[1]
    tm = _pick(s, (512, 256, 128))
    return pl.pallas_call(
        functools.partial(_router_kernel, n_experts=n_experts),
        grid=(bsz, s // tm),
        in_specs=[
            pl.BlockSpec((1, tm, d), lambda b, i: (b, i, 0)),
            pl.BlockSpec((1, N_MOD, d), lambda b, i: (b, 0, 0)),
            pl.BlockSpec((d, LANE), lambda b, i: (0, 0)),
        ],
        out_specs=pl.BlockSpec((1, tm, LANE), lambda b, i: (b, i, 0)),
        out_shape=jax.ShapeDtypeStruct((bsz, s, LANE), F32),
        compiler_params=_cparams(("parallel", "parallel")),
        name="moe_router",
    )(x, mod, _pad_cols(w_router, LANE))


def _moe_gu_kernel(x_ref, mod_ref, comb_ref, wg_ref, wu_ref, o_ref, h_scr):
    e = pl.program_id(2)

    @pl.when((e == 0) & (pl.program_id(3) == 0))
    def _():
        shift = mod_ref[0, 3:4, :]
        scale = mod_ref[0, 4:5, :]
        h_scr[...] = (x_ref[0] * (1.0 + scale) + shift).astype(BF16)

    h = h_scr[...]
    g = _dot(h, wg_ref[0])
    u = _dot(h, wu_ref[0])
    comb = comb_ref[0]
    lane = lax.broadcasted_iota(jnp.int32, comb.shape, 1)
    w = jnp.sum(jnp.where(lane == e, comb, 0.0), axis=-1, keepdims=True)
    o_ref[0] = (_silu(g) * u * w).astype(o_ref.dtype)


def moe_ffn(x, mod, w_router, w_gu, w_down, ln_w, ln_b, alpha):
    bsz, s, d = x.shape
    n_experts, _, ff2 = w_gu.shape
    ff = ff2 // 2
    comb = moe_router(x, mod, w_router)
    tm = _pick(s, (1024, 512, 256, 128))
    tn = _pick(ff, (512, 256, 128))
    nt = ff // tn
    wb = w_gu.astype(BF16)
    hmid = pl.pallas_call(
        _moe_gu_kernel,
        grid=(bsz, s // tm, n_experts, nt),
        in_specs=[
            pl.BlockSpec((1, tm, d), lambda b, i, e, j: (b, i, 0)),
            pl.BlockSpec((1, N_MOD, d), lambda b, i, e, j: (b, 0, 0)),
            pl.BlockSpec((1, tm, LANE), lambda b, i, e, j: (b, i, 0)),
            pl.BlockSpec((1, d, tn), lambda b, i, e, j: (e, 0, j)),
            pl.BlockSpec((1, d, tn), lambda b, i, e, j: (e, 0, nt + j)),
        ],
        out_specs=pl.BlockSpec((1, tm, tn), lambda b, i, e, j: (b, i, e * nt + j)),
        out_shape=jax.ShapeDtypeStruct((bsz, s, n_experts * ff), BF16),
        scratch_shapes=[pltpu.VMEM((tm, d), BF16)],
        compiler_params=_cparams(("parallel", "parallel", "arbitrary", "arbitrary")),
        name="moe_gate_up",
    )(x, mod, comb, wb, wb)
    wd = w_down.reshape(n_experts * ff, d).astype(BF16)
    return mm_res_ln(hmid, wd, x, mod, 5, ln_w, ln_b, alpha)


def _gla_tables(chunk):
    t = np.arange(chunk)[:, None]
    u = np.arange(chunk)[None, :]
    mats = [(u <= t), (u > t)]
    levels = int(math.log2(chunk))
    for l in range(1, levels + 1):
        half = 1 << (l - 1)
        anchor = ((t >> l) << l) + half - 1
        upper = (t & half) != 0
        mats.append(np.where(upper, (u > anchor) & (u <= t), (u > t) & (u <= anchor)))
    return np.concatenate(mats, axis=0).astype(np.float32), levels


def _gla_kernel(q_ref, f_ref, i_ref, g_ref, lb_ref, nw_ref, m_ref, o_ref, st_ref, *,
                chunk, levels, n_chunks):
    @pl.when(pl.program_id(2) == 0)
    def _():
        st_ref[...] = jnp.zeros_like(st_ref)

    lb = lb_ref[...]
    nw = nw_ref[...]
    row = lax.broadcasted_iota(jnp.int32, (chunk, HG_DK), 0)
    srow = lax.broadcasted_iota(jnp.int32, (chunk, chunk), 0)
    scol = lax.broadcasted_iota(jnp.int32, (chunk, chunk), 1)

    def body(c, carry):
        rows = pl.ds(pl.multiple_of(c * chunk, chunk), chunk)
        xq = q_ref[0, rows, :]
        fg = lb + (1.0 - lb) * _sigmoid(f_ref[0, rows, :])
        v = i_ref[0, rows, :]
        xg = g_ref[0, rows, :]
        logf = jnp.log(fg)
        kk = 1.0 - fg
        qq = _silu(xq)
        g_hi = logf.astype(BF16)
        g_lo = (logf - g_hi.astype(F32)).astype(BF16)
        e2 = _dot(m_ref[...], jnp.concatenate([g_hi, g_lo], axis=1))
        e = e2[:, :HG_DK] + e2[:, HG_DK:]
        b = e[0:chunk]
        b_rest = e[chunk:2 * chunk]
        st = st_ref[...]
        vb = v.astype(BF16)
        o = _dot_nt((qq * jnp.exp(b)).astype(BF16), st.astype(BF16))
        o = o + jnp.sum(qq * kk, axis=-1, keepdims=True) * v
        scores = jnp.zeros((chunk, chunk), F32)
        for l in range(1, levels + 1):
            half = 1 << (l - 1)
            a = jnp.exp(e[(l + 1) * chunk:(l + 2) * chunk])
            upper = (row & half) != 0
            qa = jnp.where(upper, qq * a, 0.0).astype(BF16)
            ka = jnp.where(upper, 0.0, kk * a).astype(BF16)
            s_l = _dot_nt(qa, ka)
            if l < levels:
                s_l = jnp.where((srow >> l) == (scol >> l), s_l, 0.0)
            scores = scores + s_l
        o = o + _dot(scores.astype(BF16), vb)
        b_end = b[chunk - 1:chunk, :]
        st_ref[...] = st * jnp.exp(b_end) + _dot_tn(vb, (kk * jnp.exp(b_rest)).astype(BF16))
        o = o * lax.rsqrt(jnp.mean(o * o, axis=-1, keepdims=True) + RMS_EPS) * nw
        o_ref[0, rows, :] = (o * _silu(xg)).astype(o_ref.dtype)
        return carry

    lax.fori_loop(0, n_chunks, body, 0)


def hgrn2_core(proj, lb, norm_w):
    bsz, s, d4 = proj.shape
    d = d4 // 4
    nh = d // HG_DK
    tt = _pick(s, (512, 256, 128, 64))
    chunk = min(GLA_CHUNK, tt)
    tables, levels = _gla_tables(chunk)
    col = lambda part: (lambda b, h, t: (b, t, part * nh + h))
    blk = (1, tt, HG_DK)
    return pl.pallas_call(
        functools.partial(_gla_kernel, chunk=chunk, levels=levels, n_chunks=tt // chunk),
        grid=(bsz, nh, s // tt),
        in_specs=[
            pl.BlockSpec(blk, col(0)),
            pl.BlockSpec(blk, col(1)),
            pl.BlockSpec(blk, col(2)),
            pl.BlockSpec(blk, col(3)),
            pl.BlockSpec((1, HG_DK), lambda b, h, t: (0, h)),
            pl.BlockSpec((1, HG_DK), lambda b, h, t: (0, 0)),
            pl.BlockSpec(tables.shape, lambda b, h, t: (0, 0)),
        ],
        out_specs=pl.BlockSpec(blk, lambda b, h, t: (b, t, h)),
        out_shape=jax.ShapeDtypeStruct((bsz, s, d), BF16),
        scratch_shapes=[pltpu.VMEM((HG_DK, HG_DK), F32)],
        compiler_params=_cparams(("parallel", "parallel", "arbitrary")),
        name="hgrn2_core",
    )(proj, proj, proj, proj, lb.reshape(1, d), norm_w.reshape(1, HG_DK),
      jnp.asarray(tables, BF16))


def _lower_bounds_kernel(l_ref, o_ref):
    x = l_ref[...]
    e = jnp.exp(x - jnp.max(x, axis=0, keepdims=True))
    p = e / jnp.sum(e, axis=0, keepdims=True)
    n = x.shape[0]
    r = lax.broadcasted_iota(jnp.int32, (n, n), 0)
    c = lax.broadcasted_iota(jnp.int32, (n, n), 1)
    acc = jnp.zeros_like(p)
    for j in range(n):
        acc = acc + jnp.where(r[:, j:j + 1] >= j, 1.0, 0.0) * p[j:j + 1, :]
    del c
    o_ref[...] = acc


def lower_bounds(logits):
    return pl.pallas_call(
        _lower_bounds_kernel,
        out_shape=jax.ShapeDtypeStruct(logits.shape, F32),
        name="hgrn2_lower_bounds",
    )(logits)


def hgrn2_layer(x, mod, w_in, lb, norm_w, w_out, ln_w, ln_b, alpha):
    proj = mm_mod(x, mod, w_in.astype(BF16), shift_row=0)
    o = hgrn2_core(proj, lb, norm_w)
    return mm_res_ln(o, w_out.astype(BF16), x, mod, 2, ln_w, ln_b, alpha)


def _split_hi_lo(v):
    hi = v.astype(BF16)
    lo = (v - hi.astype(F32)).astype(BF16)
    return hi, lo


def _ssd_kernel(xbc_ref, z_ref, dt_ref, cw_ref, cb_ref, dtb_ref, alog_ref, dsk_ref, nw_ref,
                ltri_ref, exp_ref, o_ref, st_ref, ext_ref, xc_ref, *, tc, di, groups):
    gw = di // groups
    gn = groups * SSD_DSTATE
    ch = di + 2 * gn
    halo = 8

    @pl.when(pl.program_id(1) == 0)
    def _():
        st_ref[...] = jnp.zeros_like(st_ref)
        ext_ref[0:halo, :] = jnp.zeros((halo, ch), F32)

    ext_ref[halo:halo + tc, :] = xbc_ref[0]
    cblk = _pick(ch, (512, 256, 128))
    for j in range(ch // cblk):
        cs = slice(j * cblk, (j + 1) * cblk)
        acc = cb_ref[:, cs] + cw_ref[0:1, cs] * ext_ref[halo - 3:halo - 3 + tc, cs]
        for t in range(1, SSD_CONV):
            acc = acc + cw_ref[t:t + 1, cs] * ext_ref[halo - 3 + t:halo - 3 + t + tc, cs]
        xc_ref[:, cs] = _silu(acc)
    ext_ref[0:halo, :] = ext_ref[tc:tc + halo, :]

    dt = _softplus(dt_ref[0] + dtb_ref[...])
    da = dt * (-jnp.exp(alog_ref[...]))
    da_hi, da_lo = _split_hi_lo(da)
    a2 = _dot(ltri_ref[...], jnp.concatenate([da_hi, da_lo], axis=1))
    a = a2[:, :LANE] + a2[:, LANE:]
    a_t = a.T
    a_end = a[tc - 1:tc, :]
    a_hi, a_lo = _split_hi_lo(a)
    d_hi, d_lo = _split_hi_lo(dt)
    e_hi, e_lo = _split_hi_lo(a_end)
    stack = jnp.concatenate([a_hi, a_lo, d_hi, d_lo,
                             jnp.broadcast_to(e_hi, (8, LANE)), jnp.broadcast_to(e_lo, (8, LANE))], axis=0)
    row = lax.broadcasted_iota(jnp.int32, (tc, tc), 0)
    col = lax.broadcasted_iota(jnp.int32, (tc, tc), 1)
    causal = row >= col
    lane = lax.broadcasted_iota(jnp.int32, (tc, LANE), 1)
    low_half = lane < SSD_HEADDIM

    for g in range(groups):
        gs = slice(g * gw, (g + 1) * gw)
        ex = _dot(stack, exp_ref[:, gs])
        a_x = ex[0:tc] + ex[tc:2 * tc]
        dt_x = ex[2 * tc:3 * tc] + ex[3 * tc:4 * tc]
        ae_x = ex[4 * tc:4 * tc + 1] + ex[4 * tc + 8:4 * tc + 9]
        bg = xc_ref[:, di + g * SSD_DSTATE:di + (g + 1) * SSD_DSTATE].astype(BF16)
        cg = xc_ref[:, di + gn + g * SSD_DSTATE:di + gn + (g + 1) * SSD_DSTATE].astype(BF16)
        xs = xc_ref[:, gs]
        xdt = xs * dt_x
        cbm = _dot_nt(cg, bg)
        st = st_ref[g * SSD_DSTATE:(g + 1) * SSD_DSTATE, :]
        y = jnp.exp(a_x) * _dot(cg, st.astype(BF16)) + dsk_ref[:, gs] * xs
        tiles = []
        for j in range(gw // LANE):
            xt = xdt[:, j * LANE:(j + 1) * LANE]
            acc = None
            for half in range(2):
                h = (g * gw + j * LANE) // SSD_HEADDIM + half
                seg = a[:, h:h + 1] - a_t[h:h + 1, :]
                m = (cbm * jnp.exp(jnp.where(causal, seg, -1e30))).astype(BF16)
                xm = jnp.where(low_half if half == 0 else jnp.logical_not(low_half), xt, 0.0)
                part = _dot(m, xm.astype(BF16))
                acc = part if acc is None else acc + part
            tiles.append(acc)
        y = y + jnp.concatenate(tiles, axis=1)
        xw = (xdt * jnp.exp(ae_x - a_x)).astype(BF16)
        st_ref[g * SSD_DSTATE:(g + 1) * SSD_DSTATE, :] = st * jnp.exp(ae_x) + _dot_tn(bg, xw)
        y = y * _silu(z_ref[0, :, gs])
        y = y * lax.rsqrt(jnp.mean(y * y, axis=-1, keepdims=True) + RMS_EPS) * nw_ref[:, gs]
        o_ref[0, :, gs] = y.astype(o_ref.dtype)


def ssd_core(xbc, z, dt, conv_w, conv_b, dt_bias, a_log, d_skip, norm_w):
    bsz, s, ch = xbc.shape
    di = z.shape[-1]
    nh = di // SSD_HEADDIM
    groups = (ch - di) // (2 * SSD_DSTATE)
    gw = di // groups
    tc = min(SSD_CHUNK, s)
    pad_h = lambda v: jnp.pad(v, (0, LANE - nh)).reshape(1, LANE)
    ltri = jnp.asarray(np.tril(np.ones((tc, tc), np.float32)), BF16)
    expand = np.zeros((LANE, di), np.float32)
    expand[np.arange(di) // SSD_HEADDIM, np.arange(di)] = 1.0
    full = lambda shape: pl.BlockSpec(shape, lambda b, c: (0,) * len(shape))
    return pl.pallas_call(
        functools.partial(_ssd_kernel, tc=tc, di=di, groups=groups),
        grid=(bsz, s // tc),
        in_specs=[
            pl.BlockSpec((1, tc, ch), lambda b, c: (b, c, 0)),
            pl.BlockSpec((1, tc, di), lambda b, c: (b, c, 0)),
            pl.BlockSpec((1, tc, LANE), lambda b, c: (b, c, 0)),
            full((SSD_CONV, ch)), full((1, ch)), full((1, LANE)), full((1, LANE)),
            full((1, di)), full((1, di)), full((tc, tc)), full((LANE, di)),
        ],
        out_specs=pl.BlockSpec((1, tc, di), lambda b, c: (b, c, 0)),
        out_shape=jax.ShapeDtypeStruct((bsz, s, di), BF16),
        scratch_shapes=[
            pltpu.VMEM((groups * SSD_DSTATE, gw), F32),
            pltpu.VMEM((tc + 8, ch), F32),
            pltpu.VMEM((tc, ch), F32),
        ],
        compiler_params=_cparams(("parallel", "arbitrary")),
        name="ssd_core",
    )(xbc, z, dt, conv_w, conv_b.reshape(1, ch), pad_h(dt_bias), pad_h(a_log),
      jnp.repeat(d_skip, SSD_HEADDIM).reshape(1, di), norm_w.reshape(1, di), ltri,
      jnp.asarray(expand, BF16))


def mamba2_layer(x, mod, w_in, conv_w, conv_b, dt_bias, a_log, d_skip, norm_w, w_out,
                 ln_w, ln_b, alpha):
    di = w_out.shape[0]
    ch = conv_w.shape[1]
    wb = w_in.astype(BF16)
    z = mm_mod(x, mod, wb[:, :di], shift_row=0)
    xbc = mm_mod(x, mod, wb[:, di:di + ch], shift_row=0)
    dt = mm_mod(x, mod, _pad_cols(wb[:, di + ch:], LANE), shift_row=0)
    y = ssd_core(xbc, z, dt, conv_w, conv_b, dt_bias, a_log, d_skip, norm_w)
    return mm_res_ln(y, w_out.astype(BF16), x, mod, 2, ln_w, ln_b, alpha)


def _rows(start, size, stride):
    return pl.ds(start, size) if stride == 1 else pl.ds(start, size, stride=stride)


def _da_kernel(*refs, tile, n_heads, scale):
    n_grp = len(DA_PATTERNS)
    ins = refs[:5 * n_grp]
    o_ref, m_ref, l_ref, acc_ref = refs[5 * n_grp:]
    span = DA_SPAN
    first_tile = pl.program_id(2) == 0
    head = jnp.full((1, 2 * span), pl.program_id(1), jnp.int32).astype(F32)
    slope = jnp.exp((head + 1.0) * (-8.0 * math.log(2.0) / n_heads))
    qi = lax.broadcasted_iota(jnp.int32, (span, 2 * span), 0)
    ki = lax.broadcasted_iota(jnp.int32, (span, 2 * span), 1)
    dist = qi + span - ki
    valid = (dist >= 0) & (dist <= span)
    valid_first = valid & jnp.logical_or(ki >= span, jnp.logical_not(first_tile))
    neg = -1e30

    for g, (window, dil) in enumerate(DA_PATTERNS):
        q_ref, kc_ref, vc_ref, kp_ref, vp_ref = ins[5 * g:5 * g + 5]
        bias = dist.astype(F32) * (slope * (-float(dil)))
        for r in range(dil):
            for i in range(tile // (span * dil)):
                qrows = _rows(i * span * dil + r, span, dil)
                q = (q_ref[0, qrows, :] * scale).astype(BF16)
                if i == 0:
                    prow = _rows(r, span, dil)
                    k = jnp.concatenate([kp_ref[0, prow, :], kc_ref[0, prow, :]], axis=0)
                    v = jnp.concatenate([vp_ref[0, prow, :], vc_ref[0, prow, :]], axis=0)
                    ok = valid_first
                else:
                    krows = _rows((i - 1) * span * dil + r, 2 * span, dil)
                    k = kc_ref[0, krows, :]
                    v = vc_ref[0, krows, :]
                    ok = valid
                s = jnp.where(ok, _dot_nt(q, k.astype(BF16)) + bias, neg)
                m_blk = jnp.max(s, axis=-1, keepdims=True)
                if g == 0:
                    m_new = m_blk
                    p = jnp.exp(s - m_new)
                    l_new = jnp.sum(p, axis=-1, keepdims=True)
                    acc_new = _dot(p.astype(BF16), v.astype(BF16))
                else:
                    m_old = m_ref[qrows, :]
                    m_new = jnp.maximum(m_old, m_blk)
                    corr = jnp.exp(m_old - m_new)
                    p = jnp.exp(s - m_new)
                    l_new = corr * l_ref[qrows, :] + jnp.sum(p, axis=-1, keepdims=True)
                    acc_new = corr * acc_ref[qrows, :] + _dot(p.astype(BF16), v.astype(BF16))
                m_ref[qrows, :] = m_new
                l_ref[qrows, :] = l_new
                acc_ref[qrows, :] = acc_new

    o_ref[0] = (acc_ref[...] / l_ref[...]).astype(o_ref.dtype)


def dilated_attention_core(qkv, n_heads):
    bsz, s, _ = qkv.shape
    hd = DA_HEADDIM
    tile = max(w for w, _ in DA_PATTERNS)
    assert s % tile == 0 and all(w // d == DA_SPAN for w, d in DA_PATTERNS)
    in_specs, args = [], []
    for g, (window, dil) in enumerate(DA_PATTERNS):
        prev_rows = DA_SPAN * dil
        per_tile = tile // prev_rows
        for part in (0, 1, 2):
            colb = (g * 3 + part) * n_heads
            in_specs.append(pl.BlockSpec((1, tile, hd), lambda b, h, t, colb=colb: (b, t, colb + h)))
            args.append(qkv)
            if part == 0:
                continue
        for part in (1, 2):
            colb = (g * 3 + part) * n_heads
            in_specs.append(pl.BlockSpec(
                (1, prev_rows, hd),
                lambda b, h, t, colb=colb, per_tile=per_tile: (b, jnp.maximum(t * per_tile - 1, 0), colb + h)))
            args.append(qkv)
    return pl.pallas_call(
        functools.partial(_da_kernel, tile=tile, n_heads=n_heads, scale=hd ** -0.5),
        grid=(bsz, n_heads, s // tile),
        in_specs=in_specs,
        out_specs=pl.BlockSpec((1, tile, hd), lambda b, h, t: (b, t, h)),
        out_shape=jax.ShapeDtypeStruct((bsz, s, n_heads * hd), BF16),
        scratch_shapes=[
            pltpu.VMEM((tile, 1), F32),
            pltpu.VMEM((tile, 1), F32),
            pltpu.VMEM((tile, hd), F32),
        ],
        compiler_params=_cparams(("parallel", "parallel", "arbitrary")),
        name="dilated_attention",
    )(*args)


def dilated_attention_layer(x, mod, w_in, w_out, ln_w, ln_b, alpha):
    d = x.shape[-1]
    qkv = mm_mod(x, mod, w_in.astype(BF16), shift_row=0)
    o = dilated_attention_core(qkv, d // DA_HEADDIM)
    return mm_res_ln(o, w_out.astype(BF16), x, mod, 2, ln_w, ln_b, alpha)


RW_TN = 256


def _rw_mix_kernel(x_ref, xp_ref, mod_ref, mu_ref, w_ref, o_ref, h_scr, xx_scr):
    @pl.when(pl.program_id(2) == 0)
    def _():
        shift = mod_ref[0, 0:1, :]
        scale = 1.0 + mod_ref[0, 1:2, :]
        h = x_ref[0] * scale + shift
        prev_row = xp_ref[0, 7:8, :] * scale + shift
        prev_row = jnp.where(pl.program_id(1) == 0, 0.0, prev_row)
        rolled = pltpu.roll(h, 1, 0)
        row = lax.broadcasted_iota(jnp.int32, h.shape, 0)
        h_scr[...] = h
        xx_scr[...] = jnp.where(row == 0, prev_row, rolled) - h

    mixed = (h_scr[...] + xx_scr[...] * mu_ref[0]).astype(BF16)
    o_ref[0] = _dot(mixed, w_ref[...])


def rw_mix_mm(x, mod, mu_tiles, w_all):
    bsz, s, d = x.shape
    n = w_all.shape[1]
    tm = _pick(s, (512, 256, 128))
    return pl.pallas_call(
        _rw_mix_kernel,
        grid=(bsz, s // tm, n // RW_TN),
        in_specs=[
            pl.BlockSpec((1, tm, d), lambda b, i, j: (b, i, 0)),
            pl.BlockSpec((1, 8, d), lambda b, i, j: (b, jnp.maximum(i * (tm // 8) - 1, 0), 0)),
            pl.BlockSpec((1, N_MOD, d), lambda b, i, j: (b, 0, 0)),
            pl.BlockSpec((1, 1, d), lambda b, i, j: (j, 0, 0)),
            pl.BlockSpec((d, RW_TN), lambda b, i, j: (0, j)),
        ],
        out_specs=pl.BlockSpec((1, tm, RW_TN), lambda b, i, j: (b, i, j)),
        out_shape=jax.ShapeDtypeStruct((bsz, s, n), F32),
        scratch_shapes=[pltpu.VMEM((tm, d), F32), pltpu.VMEM((tm, d), F32)],
        compiler_params=_cparams(("parallel", "parallel", "arbitrary")),
        name="rw_mix_mm",
    )(x, x, mod, mu_tiles, w_all)


def _rw_lora2_kernel(w1h_ref, a1h_ref, g1h_ref, w2_ref, a2_ref, g2_ref, w0_ref, a0_ref,
                     wp_ref, ap_ref, gt_ref):
    wp_ref[0] = w0_ref[...] + _dot(jnp.tanh(w1h_ref[0]).astype(BF16), w2_ref[...])
    ap_ref[0] = a0_ref[...] + _dot(a1h_ref[0].astype(BF16), a2_ref[...])
    gt_ref[0] = _dot(_sigmoid(g1h_ref[0]).astype(BF16), g2_ref[...])


def rw_lora2(proj, col0, w2, a2, g2, w0, a0):
    bsz, s, _ = proj.shape
    d = w2.shape[1]
    tm = _pick(s, (256, 128))
    cb = col0 // RW_TN
    hspec = lambda k: pl.BlockSpec((1, tm, RW_TN), lambda b, i, k=k: (b, i, cb + k))
    wspec = pl.BlockSpec((RW_TN, d), lambda b, i: (0, 0))
    vspec = pl.BlockSpec((1, d), lambda b, i: (0, 0))
    ospec = pl.BlockSpec((1, tm, d), lambda b, i: (b, i, 0))
    oshape = jax.ShapeDtypeStruct((bsz, s, d), F32)
    return pl.pallas_call(
        _rw_lora2_kernel,
        grid=(bsz, s // tm),
        in_specs=[hspec(0), hspec(1), hspec(2), wspec, wspec, wspec, vspec, vspec],
        out_specs=[ospec, ospec, ospec],
        out_shape=[oshape, oshape, oshape],
        compiler_params=_cparams(("parallel", "parallel")),
        name="rw_lora2",
    )(proj, proj, proj, w2, a2, g2, w0.reshape(1, d), a0.reshape(1, d))


def _rw_scan_kernel(r_ref, k_ref, v_ref, wp_ref, ap_ref, kk_ref, ka_ref, rk_ref, lnw_ref, lnb_ref,
                    o_ref, st_ref, dec_ref, kkn_ref, bb_ref, kh_ref, *, tb, lw):
    hd = RW_HEADDIM
    bh = r_ref.shape[-1]

    @pl.when(pl.program_id(0) == 0)
    def _():
        st_ref[...] = jnp.zeros_like(st_ref)

    def prep(t, carry):
        a = _sigmoid(ap_ref[t])
        kraw = k_ref[t]
        kk = kraw * kk_ref[...]
        nrm = jnp.sum(kk * kk, axis=0, keepdims=True)
        kkn = kk * lax.rsqrt(jnp.maximum(nrm, 1e-24))
        dec_ref[t] = jnp.exp(-jnp.exp(-_softplus(-wp_ref[t]) - 0.5))
        kkn_ref[t] = kkn
        bb_ref[t] = kkn * a
        kh_ref[t] = kraw * (1.0 + (a - 1.0) * ka_ref[...])
        return carry

    lax.fori_loop(0, tb, prep, 0)

    sub = 8

    def rows8(ref, t, kb, lanes):
        return ref[t, pl.ds(pl.multiple_of(kb * sub, sub), sub), lanes]

    def bcast(tile, j):
        return jnp.broadcast_to(tile[j:j + 1, :], (hd, lw))

    def step(t, carry):
        for c in range(bh // lw):
            lanes = slice(c * lw, (c + 1) * lw)

            def pass1(kb, sa):
                a8 = rows8(kkn_ref, t, kb, lanes)
                for j in range(sub):
                    sa = sa + st_ref[kb * sub + j, :, lanes] * bcast(a8, j)
                return sa

            sa = -lax.fori_loop(0, hd // sub, pass1, jnp.zeros((hd, lw), F32))
            vt = v_ref[t, :, lanes]

            def pass2(kb, y):
                d8 = rows8(dec_ref, t, kb, lanes)
                b8 = rows8(bb_ref, t, kb, lanes)
                k8 = rows8(kh_ref, t, kb, lanes)
                r8 = rows8(r_ref, t, kb, lanes)
                for j in range(sub):
                    s_new = (st_ref[kb * sub + j, :, lanes] * bcast(d8, j)
                             + sa * bcast(b8, j) + vt * bcast(k8, j))
                    st_ref[kb * sub + j, :, lanes] = s_new
                    y = y + s_new * bcast(r8, j)
                return y

            y = lax.fori_loop(0, hd // sub, pass2, jnp.zeros((hd, lw), F32))
            mu = jnp.mean(y, axis=0, keepdims=True)
            yc = y - mu
            var = jnp.mean(yc * yc, axis=0, keepdims=True)
            yn = yc * lax.rsqrt(var + RW_GN_EPS) * lnw_ref[:, lanes] + lnb_ref[:, lanes]
            bonus = jnp.sum(r_ref[t, :, lanes] * kh_ref[t, :, lanes] * rk_ref[:, lanes],
                            axis=0, keepdims=True)
            o_ref[t, :, lanes] = yn + bonus * vt
        return carry

    lax.fori_loop(0, tb, step, 0)


def rw_scan(r, k, v, wpre, apre, k_k, k_a, r_k, ln_w, ln_b):
    s, hd, bh = r.shape
    tb = _pick(s, (16, 8))
    lw = min(LANE, bh)
    seq = pl.BlockSpec((tb, hd, bh), lambda i: (i, 0, 0))
    par = pl.BlockSpec((hd, bh), lambda i: (0, 0))
    return pl.pallas_call(
        functools.partial(_rw_scan_kernel, tb=tb, lw=lw),
        grid=(s // tb,),
        in_specs=[seq] * 5 + [par] * 5,
        out_specs=seq,
        out_shape=jax.ShapeDtypeStruct((s, hd, bh), F32),
        scratch_shapes=[pltpu.VMEM((hd, hd, bh), F32)] + [pltpu.VMEM((tb, hd, bh), F32)] * 4,
        compiler_params=_cparams(("arbitrary",)),
        name="rw_scan",
    )(r, k, v, wpre, apre, k_k, k_a, r_k, ln_w, ln_b)


def rwkv7_layer(x, mod, mu, w_rkv, w0, w1, w2, a0, a1, a2, g1, g2, k_k, k_a, r_k, gn_w, gn_b,
                w_out, ln_w, ln_b, alpha):
    bsz, s, d = x.shape
    nh = d // RW_HEADDIM
    pad_c = lambda w: _pad_cols(w, RW_TN)
    pad_r = lambda w: jnp.pad(w, ((0, RW_TN - w.shape[0]), (0, 0)))
    secs = [w_rkv[0], w_rkv[1], w_rkv[2], pad_c(w1), pad_c(a1), pad_c(g1)]
    w_all = jnp.concatenate(secs, axis=1).astype(BF16)
    mu_tiles = jnp.concatenate(
        [jnp.broadcast_to(mu[i], (secs[i].shape[1] // RW_TN, d)) for i in (0, 1, 2, 3, 4, 5)],
        axis=0)[:, None, :]
    proj = rw_mix_mm(x, mod, mu_tiles, w_all)
    wpre, apre, gate = rw_lora2(proj, 3 * d, pad_r(w2).astype(BF16), pad_r(a2).astype(BF16),
                                pad_r(g2).astype(BF16), w0, a0)

    def to_scan(t):
        return t.reshape(bsz, s, nh, RW_HEADDIM).transpose(1, 3, 0, 2).reshape(s, RW_HEADDIM, bsz * nh)

    def par(p):
        return jnp.tile(p.reshape(nh, RW_HEADDIM).T[:, None, :], (1, bsz, 1)).reshape(RW_HEADDIM, bsz * nh)

    y = rw_scan(to_scan(proj[..., 0:d]), to_scan(proj[..., d:2 * d]), to_scan(proj[..., 2 * d:3 * d]),
                to_scan(wpre), to_scan(apre), par(k_k), par(k_a), par(r_k.reshape(-1)),
                par(gn_w), par(gn_b))
    y = y.reshape(s, RW_HEADDIM, bsz, nh).transpose(2, 0, 3, 1).reshape(bsz, s, d)
    return mm_res_ln(y, w_out.astype(BF16), x, mod, 2, ln_w, ln_b, alpha, h2=gate)


def kernel(x, c, ada_w, ada_b, ln_w, ln_b, hg_w_in, hg_lb_logits, hg_norm_w, hg_w_out, ssd_w_in, ssd_conv_w, ssd_conv_b, ssd_dt_bias, ssd_a_log, ssd_d, ssd_norm_w, ssd_w_out, da_w_in, da_w_out, rw_mu, rw_w_rkv, rw_w0, rw_w1, rw_w2, rw_a0, rw_a1, rw_a2, rw_g1, rw_g2, rw_k_k, rw_k_a, rw_r_k, rw_ln_w, rw_ln_b, rw_w_out, ffn_w_gu, ffn_w_down, moe_router, moe_w_gu, moe_w_down):
    depth = ada_w.shape[0]
    bsz, _, d = x.shape
    alpha = (2.0 * depth) ** 0.25
    lbs = lower_bounds(hg_lb_logits)
    mods = ada_modulation(c, ada_w, ada_b).reshape(depth, bsz, N_MOD, d)
    for i in range(depth):
        mixer, j = i % 4, i // 4
        mod = mods[i]
        if mixer == 0:
            x = hgrn2_layer(x, mod, hg_w_in[j], lbs[i], hg_norm_w[j], hg_w_out[j],
                            ln_w[i, 0], ln_b[i, 0], alpha)
        elif mixer == 1:
            x = mamba2_layer(x, mod, ssd_w_in[j], ssd_conv_w[j], ssd_conv_b[j], ssd_dt_bias[j],
                             ssd_a_log[j], ssd_d[j], ssd_norm_w[j], ssd_w_out[j],
                             ln_w[i, 0], ln_b[i, 0], alpha)
        elif mixer == 2:
            x = dilated_attention_layer(x, mod, da_w_in[j], da_w_out[j], ln_w[i, 0], ln_b[i, 0], alpha)
        else:
            x = rwkv7_layer(x, mod, rw_mu[j], rw_w_rkv[j], rw_w0[j], rw_w1[j], rw_w2[j], rw_a0[j],
                            rw_a1[j], rw_a2[j], rw_g1[j], rw_g2[j], rw_k_k[j], rw_k_a[j], rw_r_k[j],
                            rw_ln_w[j], rw_ln_b[j], rw_w_out[j], ln_w[i, 0], ln_b[i, 0], alpha)
        if i % 2 == 0:
            x = dense_ffn(x, mod, ffn_w_gu[i // 2], ffn_w_down[i // 2], ln_w[i, 1], ln_b[i, 1], alpha)
        else:
            x = moe_ffn(x, mod, moe_router[i // 2], moe_w_gu[i // 2], moe_w_down[i // 2],
                        ln_w[i, 1], ln_b[i, 1], alpha)
    return x
```

```python
import functools
import math

import numpy as np
import jax
import jax.numpy as jnp
from jax import lax
from jax.experimental import pallas as pl
from jax.experimental.pallas import tpu as pltpu

F32 = jnp.float32
BF16 = jnp.bfloat16

N_MOD = 6
LN_EPS = 1e-5
RMS_EPS = 1e-6

HG_DK = 128
GLA_CHUNK = 64
SSD_HEADDIM = 64
SSD_DSTATE = 128
SSD_CONV = 4
SSD_CHUNK = 256
DA_PATTERNS = ((128, 1), (512, 4), (2048, 16))
DA_HEADDIM = 128
DA_SPAN = 128
RW_HEADDIM = 64
RW_GN_EPS = 64e-5
MOE_TOPK = 2

LANE = 128
VMEM_LIMIT = 56 * 1024 * 1024


def _pick(n, cands):
    for c in cands:
        if n % c == 0:
            return c
    return n


def _cparams(sem):
    return pltpu.CompilerParams(dimension_semantics=sem, vmem_limit_bytes=VMEM_LIMIT)


def _sigmoid(x):
    return 1.0 / (1.0 + jnp.exp(-x))


def _silu(x):
    return x * _sigmoid(x)


def _softplus(x):
    return jnp.maximum(x, 0.0) + jnp.log(1.0 + jnp.exp(-jnp.abs(x)))


def _dot(a, b):
    return jnp.dot(a, b, preferred_element_type=F32)


def _dot_nt(a, b):
    return lax.dot_general(a, b, (((1,), (1,)), ((), ())), preferred_element_type=F32)


def _dot_tn(a, b):
    return lax.dot_general(a, b, (((0,), (0,)), ((), ())), preferred_element_type=F32)


def _ada_kernel(c_ref, w_ref, b_ref, o_ref):
    o_ref[0] = _dot(c_ref[...].astype(BF16), w_ref[0].astype(BF16)) + b_ref[0]


def ada_modulation(c, ada_w, ada_b):
    depth, d, n = ada_w.shape
    bsz = c.shape[0]
    tn = _pick(n, (1024, 512, 256, 128))
    return pl.pallas_call(
        _ada_kernel,
        grid=(depth, n // tn),
        in_specs=[
            pl.BlockSpec((bsz, d), lambda i, j: (0, 0)),
            pl.BlockSpec((1, d, tn), lambda i, j: (i, 0, j)),
            pl.BlockSpec((1, 1, tn), lambda i, j: (i, 0, j)),
        ],
        out_specs=pl.BlockSpec((1, bsz, tn), lambda i, j: (i, 0, j)),
        out_shape=jax.ShapeDtypeStruct((depth, bsz, n), F32),
        compiler_params=_cparams(("parallel", "parallel")),
        name="ada_modulation",
    )(c, ada_w, ada_b.reshape(depth, 1, n))


def _mm_mod_kernel(x_ref, mod_ref, w_ref, o_ref, h_scr, *, shift_row):
    @pl.when(pl.program_id(2) == 0)
    def _():
        shift = mod_ref[0, shift_row:shift_row + 1, :]
        scale = mod_ref[0, shift_row + 1:shift_row + 2, :]
        h_scr[...] = (x_ref[0] * (1.0 + scale) + shift).astype(BF16)

    o_ref[0] = _dot(h_scr[...], w_ref[...]).astype(o_ref.dtype)


def mm_mod(x, mod, w, shift_row, out_dtype=F32):
    bsz, s, d = x.shape
    n = w.shape[1]
    tm = _pick(s, (1024, 512, 256, 128))
    tn = _pick(n, (512, 256, 128))
    return pl.pallas_call(
        functools.partial(_mm_mod_kernel, shift_row=shift_row),
        grid=(bsz, s // tm, n // tn),
        in_specs=[
            pl.BlockSpec((1, tm, d), lambda b, i, j: (b, i, 0)),
            pl.BlockSpec((1, N_MOD, d), lambda b, i, j: (b, 0, 0)),
            pl.BlockSpec((d, tn), lambda b, i, j: (0, j)),
        ],
        out_specs=pl.BlockSpec((1, tm, tn), lambda b, i, j: (b, i, j)),
        out_shape=jax.ShapeDtypeStruct((bsz, s, n), out_dtype),
        scratch_shapes=[pltpu.VMEM((tm, d), BF16)],
        compiler_params=_cparams(("parallel", "parallel", "arbitrary")),
        name="mm_mod",
    )(x, mod, w)


def _mm_res_ln_kernel(*refs, gate_row, has_h2, alpha):
    if has_h2:
        h_ref, h2_ref, w_ref, x_ref, mod_ref, lnw_ref, lnb_ref, o_ref = refs
        h = (h_ref[0] * h2_ref[0]).astype(BF16)
    else:
        h_ref, w_ref, x_ref, mod_ref, lnw_ref, lnb_ref, o_ref = refs
        h = h_ref[0].astype(BF16)
    gate = mod_ref[0, gate_row:gate_row + 1, :]
    z = alpha * x_ref[0] + (1.0 + gate) * _dot(h, w_ref[...])
    mu = jnp.mean(z, axis=-1, keepdims=True)
    zc = z - mu
    var = jnp.mean(zc * zc, axis=-1, keepdims=True)
    o_ref[0] = zc * lax.rsqrt(var + LN_EPS) * lnw_ref[...] + lnb_ref[...]


def mm_res_ln(h, w, x, mod, gate_row, ln_w, ln_b, alpha, h2=None):
    bsz, s, d = x.shape
    kdim = h.shape[-1]
    tm = _pick(s, (512, 256, 128) if kdim <= 2 * d else (256, 128))
    h_spec = pl.BlockSpec((1, tm, kdim), lambda b, i: (b, i, 0))
    ins = [h] + ([h2] if h2 is not None else [])
    in_specs = [h_spec] * len(ins) + [
        pl.BlockSpec((kdim, d), lambda b, i: (0, 0), pipeline_mode=pl.Buffered(1)),
        pl.BlockSpec((1, tm, d), lambda b, i: (b, i, 0)),
        pl.BlockSpec((1, N_MOD, d), lambda b, i: (b, 0, 0)),
        pl.BlockSpec((1, d), lambda b, i: (0, 0)),
        pl.BlockSpec((1, d), lambda b, i: (0, 0)),
    ]
    return pl.pallas_call(
        functools.partial(_mm_res_ln_kernel, gate_row=gate_row, has_h2=h2 is not None, alpha=alpha),
        grid=(bsz, s // tm),
        in_specs=in_specs,
        out_specs=pl.BlockSpec((1, tm, d), lambda b, i: (b, i, 0)),
        out_shape=jax.ShapeDtypeStruct((bsz, s, d), F32),
        compiler_params=_cparams(("parallel", "parallel")),
        name="mm_res_ln",
    )(*ins, w, x, mod, ln_w.reshape(1, d), ln_b.reshape(1, d))


def _mm_swiglu_kernel(x_ref, mod_ref, wg_ref, wu_ref, o_ref, h_scr, *, shift_row):
    @pl.when(pl.program_id(2) == 0)
    def _():
        shift = mod_ref[0, shift_row:shift_row + 1, :]
        scale = mod_ref[0, shift_row + 1:shift_row + 2, :]
        h_scr[...] = (x_ref[0] * (1.0 + scale) + shift).astype(BF16)

    h = h_scr[...]
    g = _dot(h, wg_ref[...])
    u = _dot(h, wu_ref[...])
    o_ref[0] = (_silu(g) * u).astype(o_ref.dtype)


def mm_swiglu(x, mod, wg, wu, shift_row):
    bsz, s, d = x.shape
    f = wg.shape[1]
    tm = _pick(s, (1024, 512, 256, 128))
    tn = _pick(f, (512, 256, 128))
    return pl.pallas_call(
        functools.partial(_mm_swiglu_kernel, shift_row=shift_row),
        grid=(bsz, s // tm, f // tn),
        in_specs=[
            pl.BlockSpec((1, tm, d), lambda b, i, j: (b, i, 0)),
            pl.BlockSpec((1, N_MOD, d), lambda b, i, j: (b, 0, 0)),
            pl.BlockSpec((d, tn), lambda b, i, j: (0, j)),
            pl.BlockSpec((d, tn), lambda b, i, j: (0, j)),
        ],
        out_specs=pl.BlockSpec((1, tm, tn), lambda b, i, j: (b, i, j)),
        out_shape=jax.ShapeDtypeStruct((bsz, s, f), BF16),
        scratch_shapes=[pltpu.VMEM((tm, d), BF16)],
        compiler_params=_cparams(("parallel", "parallel", "arbitrary")),
        name="mm_swiglu",
    )(x, mod, wg, wu)


def _pad_cols(w, mult):
    n = w.shape[-1]
    pad = (-n) % mult
    return jnp.pad(w, ((0, 0), (0, pad))) if pad else w


def dense_ffn(x, mod, w_gu, w_down, ln_w, ln_b, alpha):
    f = w_gu.shape[1] // 2
    fmult = 512 if f >= 512 else LANE
    wg = _pad_cols(w_gu[:, :f], fmult).astype(BF16)
    wu = _pad_cols(w_gu[:, f:], fmult).astype(BF16)
    wd = jnp.pad(w_down, ((0, wg.shape[1] - f), (0, 0))).astype(BF16)
    hmid = mm_swiglu(x, mod, wg, wu, shift_row=3)
    return mm_res_ln(hmid, wd, x, mod, 5, ln_w, ln_b, alpha)


MOE_TM = 512
SEL_I1, SEL_I2, SEL_G1, SEL_G2, SEL_R1, SEL_R2 = range(6)


def _route_kernel(x_ref, mod_ref, wr_ref, ltri_ref, sel_ref, hm_ref, cnt_ref, cnt_scr, *, n_experts):
    @pl.when((pl.program_id(0) == 0) & (pl.program_id(1) == 0))
    def _():
        cnt_scr[...] = jnp.zeros_like(cnt_scr)

    shift = mod_ref[0, 3:4, :]
    scale = mod_ref[0, 4:5, :]
    h = x_ref[0] * (1.0 + scale) + shift
    hm_ref[0] = h
    logits = jnp.dot(h, wr_ref[...], preferred_element_type=F32, precision=lax.Precision.HIGHEST)
    lane = lax.broadcasted_iota(jnp.int32, logits.shape, 1)
    ninf = -jnp.inf
    lg = jnp.where(lane < n_experts, logits, ninf)
    m1 = jnp.max(lg, axis=-1, keepdims=True)
    i1 = jnp.min(jnp.where(lg == m1, lane, LANE), axis=-1, keepdims=True)
    lg2 = jnp.where(lane == i1, ninf, lg)
    m2 = jnp.max(lg2, axis=-1, keepdims=True)
    i2 = jnp.min(jnp.where(lg2 == m2, lane, LANE), axis=-1, keepdims=True)
    e2 = jnp.exp(m2 - m1)
    g1 = 1.0 / (1.0 + e2)
    hot1 = lane == i1
    hot2 = lane == i2
    chosen = jnp.where(hot1 | hot2, 1.0, 0.0)
    before = _dot(ltri_ref[...], chosen.astype(BF16)) + cnt_scr[...]
    r1 = jnp.sum(jnp.where(hot1, before, 0.0), axis=-1, keepdims=True)
    r2 = jnp.sum(jnp.where(hot2, before, 0.0), axis=-1, keepdims=True)
    cnt_scr[...] += jnp.sum(chosen, axis=0, keepdims=True)
    rec = jnp.zeros_like(logits)
    for slot, val in ((SEL_I1, i1.astype(F32)), (SEL_I2, i2.astype(F32)), (SEL_G1, g1),
                      (SEL_G2, e2 * g1), (SEL_R1, r1), (SEL_R2, r2)):
        rec = jnp.where(lane == slot, val, rec)
    sel_ref[0] = rec
    cnt_ref[...] = jnp.broadcast_to(cnt_scr[...], cnt_ref.shape)


def moe_route(x, mod, w_router):
    bsz, s, d = x.shape
    n_experts = w_router.shape[1]
    tm = _pick(s, (256, 128))
    ltri = jnp.asarray(np.tril(np.ones((tm, tm), np.float32), -1), BF16)
    return pl.pallas_call(
        functools.partial(_route_kernel, n_experts=n_experts),
        grid=(bsz, s // tm),
        in_specs=[
            pl.BlockSpec((1, tm, d), lambda b, i: (b, i, 0)),
            pl.BlockSpec((1, N_MOD, d), lambda b, i: (b, 0, 0)),
            pl.BlockSpec((d, LANE), lambda b, i: (0, 0)),
            pl.BlockSpec((tm, tm), lambda b, i: (0, 0)),
        ],
        out_specs=[
            pl.BlockSpec((1, tm, LANE), lambda b, i: (b, i, 0)),
            pl.BlockSpec((1, tm, d), lambda b, i: (b, i, 0)),
            pl.BlockSpec((8, LANE), lambda b, i: (0, 0)),
        ],
        out_shape=[
            jax.ShapeDtypeStruct((bsz, s, LANE), F32),
            jax.ShapeDtypeStruct((bsz, s, d), F32),
            jax.ShapeDtypeStruct((8, LANE), F32),
        ],
        scratch_shapes=[pltpu.VMEM((1, LANE), F32)],
        compiler_params=_cparams(("arbitrary", "arbitrary")),
        name="moe_route",
    )(x, mod, _pad_cols(w_router, LANE), ltri)


def _row_copy(src_ref, src_row, dst_ref, dst_row, sem):
    return pltpu.make_async_copy(src_ref.at[pl.ds(src_row, 1)], dst_ref.at[pl.ds(dst_row, 1)], sem)


def _dispatch_kernel(pos_ref, hm_ref, zero_ref, hs_ref, sem, *, tm):
    del zero_ref
    src = hm_ref.at[0]

    def start(r, c):
        for k in range(MOE_TOPK):
            _row_copy(src, r, hs_ref, pos_ref[0, 0, MOE_TOPK * r + k], sem).start()
        return c

    def wait(r, c):
        for k in range(MOE_TOPK):
            _row_copy(src, r, hs_ref, pos_ref[0, 0, MOE_TOPK * r + k], sem).wait()
        return c

    lax.fori_loop(0, tm, start, 0)
    lax.fori_loop(0, tm, wait, 0)


def moe_dispatch(hm, pos, n_rows):
    bsz, s, d = hm.shape
    tm = _pick(s, (256, 128))
    nt = s // tm
    pos_t = pos.reshape(bsz * nt, 1, MOE_TOPK * tm)
    return pl.pallas_call(
        functools.partial(_dispatch_kernel, tm=tm),
        grid=(bsz, nt),
        in_specs=[
            pl.BlockSpec((1, 1, MOE_TOPK * tm), lambda b, i: (b * nt + i, 0, 0), memory_space=pltpu.SMEM),
            pl.BlockSpec((1, tm, d), lambda b, i: (b, i, 0)),
            pl.BlockSpec(memory_space=pl.ANY),
        ],
        out_specs=pl.BlockSpec(memory_space=pl.ANY),
        out_shape=jax.ShapeDtypeStruct((n_rows, d), F32),
        scratch_shapes=[pltpu.SemaphoreType.DMA(())],
        input_output_aliases={2: 0},
        compiler_params=_cparams(("arbitrary", "arbitrary")),
        name="moe_dispatch",
    )(pos_t, hm, jnp.zeros((n_rows, d), F32))


def _moe_gu_kernel(te_ref, nu_ref, hs_ref, wg_ref, wu_ref, o_ref, h_scr):
    del te_ref

    @pl.when(pl.program_id(0) < nu_ref[0])
    def _():
        @pl.when(pl.program_id(1) == 0)
        def _():
            h_scr[...] = hs_ref[...].astype(BF16)

        h = h_scr[...]
        g = _dot(h, wg_ref[0])
        u = _dot(h, wu_ref[0])
        o_ref[...] = (_silu(g) * u).astype(o_ref.dtype)

    @pl.when(pl.program_id(0) >= nu_ref[0])
    def _():
        o_ref[...] = jnp.zeros_like(o_ref)


def _moe_down_kernel(te_ref, nu_ref, h_ref, w_ref, o_ref):
    del te_ref

    @pl.when(pl.program_id(1) < nu_ref[0])
    def _():
        o_ref[...] = _dot(h_ref[...], w_ref[0])

    @pl.when(pl.program_id(1) >= nu_ref[0])
    def _():
        o_ref[...] = jnp.zeros_like(o_ref)


def _moe_combine_kernel(pos_ref, ys_ref, sel_ref, x_ref, mod_ref, lnw_ref, lnb_ref, o_ref, buf, sem, *,
                        tm, alpha):
    def start(r, c):
        for k in range(MOE_TOPK):
            _row_copy(ys_ref, pos_ref[0, 0, MOE_TOPK * r + k], buf.at[k], r, sem).start()
        return c

    def wait(r, c):
        for k in range(MOE_TOPK):
            _row_copy(ys_ref, pos_ref[0, 0, MOE_TOPK * r + k], buf.at[k], r, sem).wait()
        return c

    lax.fori_loop(0, tm, start, 0)
    lax.fori_loop(0, tm, wait, 0)
    sel = sel_ref[0]
    lane = lax.broadcasted_iota(jnp.int32, sel.shape, 1)
    g1 = jnp.sum(jnp.where(lane == SEL_G1, sel, 0.0), axis=-1, keepdims=True)
    g2 = jnp.sum(jnp.where(lane == SEL_G2, sel, 0.0), axis=-1, keepdims=True)
    y = g1 * buf[0] + g2 * buf[1]
    z = alpha * x_ref[0] + (1.0 + mod_ref[0, 5:6, :]) * y
    mu = jnp.mean(z, axis=-1, keepdims=True)
    zc = z - mu
    var = jnp.mean(zc * zc, axis=-1, keepdims=True)
    o_ref[0] = zc * lax.rsqrt(var + LN_EPS) * lnw_ref[...] + lnb_ref[...]


def moe_ffn(x, mod, w_router, w_gu, w_down, ln_w, ln_b, alpha):
    bsz, s, d = x.shape
    n_experts, _, ff2 = w_gu.shape
    ff = ff2 // 2
    tmr = min(MOE_TM, s)
    n_rows = MOE_TOPK * bsz * s + n_experts * tmr
    n_tiles = n_rows // tmr

    sel, hm, cnt = moe_route(x, mod, w_router)

    counts = cnt[0, :n_experts].astype(jnp.int32)
    padded = (counts + tmr - 1) // tmr * tmr
    ends = jnp.cumsum(padded)
    offs = ends - padded
    tile_expert = jnp.minimum(
        jnp.searchsorted(ends, jnp.arange(n_tiles, dtype=jnp.int32) * tmr, side="right"),
        n_experts - 1).astype(jnp.int32)
    n_used = (ends[-1:] // tmr).astype(jnp.int32)
    ids = sel[..., SEL_I1:SEL_I2 + 1].astype(jnp.int32)
    ranks = sel[..., SEL_R1:SEL_R2 + 1].astype(jnp.int32)
    pos = offs[ids] + ranks

    hs = moe_dispatch(hm, pos, n_rows)

    tn = _pick(ff, (512, 256, 128))
    nt = ff // tn
    wb = w_gu.astype(BF16)
    live = lambda i, nu: jnp.minimum(i, nu[0] - 1)
    col = lambda i, j, nu: jnp.where(i < nu[0], j, nt - 1)
    hmid = pl.pallas_call(
        _moe_gu_kernel,
        grid_spec=pltpu.PrefetchScalarGridSpec(
            num_scalar_prefetch=2,
            grid=(n_tiles, nt),
            in_specs=[
                pl.BlockSpec((tmr, d), lambda i, j, te, nu: (live(i, nu), 0)),
                pl.BlockSpec((1, d, tn), lambda i, j, te, nu: (te[live(i, nu)], 0, col(i, j, nu))),
                pl.BlockSpec((1, d, tn), lambda i, j, te, nu: (te[live(i, nu)], 0, nt + col(i, j, nu))),
            ],
            out_specs=pl.BlockSpec((tmr, tn), lambda i, j, te, nu: (i, j)),
            scratch_shapes=[pltpu.VMEM((tmr, d), BF16)],
        ),
        out_shape=jax.ShapeDtypeStruct((n_rows, ff), BF16),
        compiler_params=_cparams(("arbitrary", "arbitrary")),
        name="moe_gate_up",
    )(tile_expert, n_used, hs, wb, wb)

    dn = _pick(d, (1024, 512, 256, 128))
    ys = pl.pallas_call(
        _moe_down_kernel,
        grid_spec=pltpu.PrefetchScalarGridSpec(
            num_scalar_prefetch=2,
            grid=(d // dn, n_tiles),
            in_specs=[
                pl.BlockSpec((tmr, ff), lambda n, i, te, nu: (live(i, nu), 0)),
                pl.BlockSpec((1, ff, dn), lambda n, i, te, nu: (te[live(i, nu)], 0, n)),
            ],
            out_specs=pl.BlockSpec((tmr, dn), lambda n, i, te, nu: (i, n)),
        ),
        out_shape=jax.ShapeDtypeStruct((n_rows, d), F32),
        compiler_params=_cparams(("arbitrary", "arbitrary")),
        name="moe_down",
    )(tile_expert, n_used, hmid, w_down.astype(BF16))

    tm = _pick(s, (256, 128))
    ntk = s // tm
    return pl.pallas_call(
        functools.partial(_moe_combine_kernel, tm=tm, alpha=alpha),
        grid=(bsz, ntk),
        in_specs=[
            pl.BlockSpec((1, 1, MOE_TOPK * tm), lambda b, i: (b * ntk + i, 0, 0), memory_space=pltpu.SMEM),
            pl.BlockSpec(memory_space=pl.ANY),
            pl.BlockSpec((1, tm, LANE), lambda b, i: (b, i, 0)),
            pl.BlockSpec((1, tm, d), lambda b, i: (b, i, 0)),
            pl.BlockSpec((1, N_MOD, d), lambda b, i: (b, 0, 0)),
            pl.BlockSpec((1, d), lambda b, i: (0, 0)),
            pl.BlockSpec((1, d), lambda b, i: (0, 0)),
        ],
        out_specs=pl.BlockSpec((1, tm, d), lambda b, i: (b, i, 0)),
        out_shape=jax.ShapeDtypeStruct((bsz, s, d), F32),
        scratch_shapes=[pltpu.VMEM((MOE_TOPK, tm, d), F32), pltpu.SemaphoreType.DMA(())],
        compiler_params=_cparams(("arbitrary", "arbitrary")),
        name="moe_combine",
    )(pos.reshape(bsz * ntk, 1, MOE_TOPK * tm), ys, sel, x, mod, ln_w.reshape(1, d), ln_b.reshape(1, d))


def _gla_tables(chunk):
    t = np.arange(chunk)[:, None]
    u = np.arange(chunk)[None, :]
    mats = [(u <= t), (u > t)]
    levels = int(math.log2(chunk))
    for l in range(1, levels + 1):
        half = 1 << (l - 1)
        anchor = ((t >> l) << l) + half - 1
        upper = (t & half) != 0
        mats.append(np.where(upper, (u > anchor) & (u <= t), (u > t) & (u <= anchor)))
    return np.concatenate(mats, axis=0).astype(np.float32), levels


def _gla_kernel(q_ref, f_ref, i_ref, g_ref, lb_ref, nw_ref, m_ref, o_ref, st_ref, *,
                chunk, levels, n_chunks):
    @pl.when(pl.program_id(2) == 0)
    def _():
        st_ref[...] = jnp.zeros_like(st_ref)

    lb = lb_ref[...]
    nw = nw_ref[...]
    row = lax.broadcasted_iota(jnp.int32, (chunk, HG_DK), 0)
    srow = lax.broadcasted_iota(jnp.int32, (chunk, chunk), 0)
    scol = lax.broadcasted_iota(jnp.int32, (chunk, chunk), 1)

    def body(c, carry):
        rows = pl.ds(pl.multiple_of(c * chunk, chunk), chunk)
        xq = q_ref[0, rows, :]
        fg = lb + (1.0 - lb) * _sigmoid(f_ref[0, rows, :])
        v = i_ref[0, rows, :]
        xg = g_ref[0, rows, :]
        logf = jnp.log(fg)
        kk = 1.0 - fg
        qq = _silu(xq)
        g_hi = logf.astype(BF16)
        g_lo = (logf - g_hi.astype(F32)).astype(BF16)
        e2 = _dot(m_ref[...], jnp.concatenate([g_hi, g_lo], axis=1))
        e = e2[:, :HG_DK] + e2[:, HG_DK:]
        b = e[0:chunk]
        b_rest = e[chunk:2 * chunk]
        st = st_ref[...]
        vb = v.astype(BF16)
        o = _dot_nt((qq * jnp.exp(b)).astype(BF16), st.astype(BF16))
        o = o + jnp.sum(qq * kk, axis=-1, keepdims=True) * v
        scores = jnp.zeros((chunk, chunk), F32)
        for l in range(1, levels + 1):
            half = 1 << (l - 1)
            a = jnp.exp(e[(l + 1) * chunk:(l + 2) * chunk])
            upper = (row & half) != 0
            qa = jnp.where(upper, qq * a, 0.0).astype(BF16)
            ka = jnp.where(upper, 0.0, kk * a).astype(BF16)
            s_l = _dot_nt(qa, ka)
            if l < levels:
                s_l = jnp.where((srow >> l) == (scol >> l), s_l, 0.0)
            scores = scores + s_l
        o = o + _dot(scores.astype(BF16), vb)
        b_end = b[chunk - 1:chunk, :]
        st_ref[...] = st * jnp.exp(b_end) + _dot_tn(vb, (kk * jnp.exp(b_rest)).astype(BF16))
        o = o * lax.rsqrt(jnp.mean(o * o, axis=-1, keepdims=True) + RMS_EPS) * nw
        o_ref[0, rows, :] = (o * _silu(xg)).astype(o_ref.dtype)
        return carry

    lax.fori_loop(0, n_chunks, body, 0, unroll=2)


def hgrn2_core(proj, lb, norm_w):
    bsz, s, d4 = proj.shape
    d = d4 // 4
    nh = d // HG_DK
    tt = _pick(s, (512, 256, 128, 64))
    chunk = min(GLA_CHUNK, tt)
    tables, levels = _gla_tables(chunk)
    col = lambda part: (lambda b, h, t: (b, t, part * nh + h))
    blk = (1, tt, HG_DK)
    return pl.pallas_call(
        functools.partial(_gla_kernel, chunk=chunk, levels=levels, n_chunks=tt // chunk),
        grid=(bsz, nh, s // tt),
        in_specs=[
            pl.BlockSpec(blk, col(0)),
            pl.BlockSpec(blk, col(1)),
            pl.BlockSpec(blk, col(2)),
            pl.BlockSpec(blk, col(3)),
            pl.BlockSpec((1, HG_DK), lambda b, h, t: (0, h)),
            pl.BlockSpec((1, HG_DK), lambda b, h, t: (0, 0)),
            pl.BlockSpec(tables.shape, lambda b, h, t: (0, 0)),
        ],
        out_specs=pl.BlockSpec(blk, lambda b, h, t: (b, t, h)),
        out_shape=jax.ShapeDtypeStruct((bsz, s, d), BF16),
        scratch_shapes=[pltpu.VMEM((HG_DK, HG_DK), F32)],
        compiler_params=_cparams(("parallel", "parallel", "arbitrary")),
        name="hgrn2_core",
    )(proj, proj, proj, proj, lb.reshape(1, d), norm_w.reshape(1, HG_DK),
      jnp.asarray(tables, BF16))


def _lower_bounds_kernel(l_ref, o_ref):
    x = l_ref[...]
    e = jnp.exp(x - jnp.max(x, axis=0, keepdims=True))
    p = e / jnp.sum(e, axis=0, keepdims=True)
    n = x.shape[0]
    r = lax.broadcasted_iota(jnp.int32, (n, n), 0)
    c = lax.broadcasted_iota(jnp.int32, (n, n), 1)
    acc = jnp.zeros_like(p)
    for j in range(n):
        acc = acc + jnp.where(r[:, j:j + 1] >= j, 1.0, 0.0) * p[j:j + 1, :]
    del c
    o_ref[...] = acc


def lower_bounds(logits):
    return pl.pallas_call(
        _lower_bounds_kernel,
        out_shape=jax.ShapeDtypeStruct(logits.shape, F32),
        name="hgrn2_lower_bounds",
    )(logits)


def hgrn2_layer(x, mod, w_in, lb, norm_w, w_out, ln_w, ln_b, alpha):
    proj = mm_mod(x, mod, w_in.astype(BF16), shift_row=0)
    o = hgrn2_core(proj, lb, norm_w)
    return mm_res_ln(o, w_out.astype(BF16), x, mod, 2, ln_w, ln_b, alpha)


def _split_hi_lo(v):
    hi = v.astype(BF16)
    lo = (v - hi.astype(F32)).astype(BF16)
    return hi, lo


def _ssd_kernel(xbc_ref, z_ref, dt_ref, cw_ref, cb_ref, dtb_ref, alog_ref, dsk_ref, nw_ref,
                ltri_ref, exp_ref, o_ref, st_ref, ext_ref, xc_ref, *, tc, di, groups):
    gw = di // groups
    gn = groups * SSD_DSTATE
    ch = di + 2 * gn
    halo = 8

    @pl.when(pl.program_id(1) == 0)
    def _():
        st_ref[...] = jnp.zeros_like(st_ref)
        ext_ref[0:halo, :] = jnp.zeros((halo, ch), F32)

    ext_ref[halo:halo + tc, :] = xbc_ref[0]
    cblk = _pick(ch, (512, 256, 128))
    for j in range(ch // cblk):
        cs = slice(j * cblk, (j + 1) * cblk)
        acc = cb_ref[:, cs] + cw_ref[0:1, cs] * ext_ref[halo - 3:halo - 3 + tc, cs]
        for t in range(1, SSD_CONV):
            acc = acc + cw_ref[t:t + 1, cs] * ext_ref[halo - 3 + t:halo - 3 + t + tc, cs]
        xc_ref[:, cs] = _silu(acc)
    ext_ref[0:halo, :] = ext_ref[tc:tc + halo, :]

    dt = _softplus(dt_ref[0] + dtb_ref[...])
    da = dt * (-jnp.exp(alog_ref[...]))
    da_hi, da_lo = _split_hi_lo(da)
    a2 = _dot(ltri_ref[...], jnp.concatenate([da_hi, da_lo], axis=1))
    a = a2[:, :LANE] + a2[:, LANE:]
    a_t = a.T
    a_end = a[tc - 1:tc, :]
    a_hi, a_lo = _split_hi_lo(a)
    d_hi, d_lo = _split_hi_lo(dt)
    e_hi, e_lo = _split_hi_lo(a_end)
    stack = jnp.concatenate([a_hi, a_lo, d_hi, d_lo,
                             jnp.broadcast_to(e_hi, (8, LANE)), jnp.broadcast_to(e_lo, (8, LANE))], axis=0)
    row = lax.broadcasted_iota(jnp.int32, (tc, tc), 0)
    col = lax.broadcasted_iota(jnp.int32, (tc, tc), 1)
    causal = row >= col
    lane = lax.broadcasted_iota(jnp.int32, (tc, LANE), 1)
    low_half = lane < SSD_HEADDIM

    for g in range(groups):
        gs = slice(g * gw, (g + 1) * gw)
        ex = _dot(stack, exp_ref[:, gs])
        a_x = ex[0:tc] + ex[tc:2 * tc]
        dt_x = ex[2 * tc:3 * tc] + ex[3 * tc:4 * tc]
        ae_x = ex[4 * tc:4 * tc + 1] + ex[4 * tc + 8:4 * tc + 9]
        bg = xc_ref[:, di + g * SSD_DSTATE:di + (g + 1) * SSD_DSTATE].astype(BF16)
        cg = xc_ref[:, di + gn + g * SSD_DSTATE:di + gn + (g + 1) * SSD_DSTATE].astype(BF16)
        xs = xc_ref[:, gs]
        xdt = xs * dt_x
        cbm = _dot_nt(cg, bg)
        st = st_ref[g * SSD_DSTATE:(g + 1) * SSD_DSTATE, :]
        y = jnp.exp(a_x) * _dot(cg, st.astype(BF16)) + dsk_ref[:, gs] * xs
        tiles = []
        for j in range(gw // LANE):
            xt = xdt[:, j * LANE:(j + 1) * LANE]
            acc = None
            for half in range(2):
                h = (g * gw + j * LANE) // SSD_HEADDIM + half
                seg = a[:, h:h + 1] - a_t[h:h + 1, :]
                m = (cbm * jnp.exp(jnp.where(causal, seg, -1e30))).astype(BF16)
                xm = jnp.where(low_half if half == 0 else jnp.logical_not(low_half), xt, 0.0)
                part = _dot(m, xm.astype(BF16))
                acc = part if acc is None else acc + part
            tiles.append(acc)
        y = y + jnp.concatenate(tiles, axis=1)
        xw = (xdt * jnp.exp(ae_x - a_x)).astype(BF16)
        st_ref[g * SSD_DSTATE:(g + 1) * SSD_DSTATE, :] = st * jnp.exp(ae_x) + _dot_tn(bg, xw)
        y = y * _silu(z_ref[0, :, gs])
        y = y * lax.rsqrt(jnp.mean(y * y, axis=-1, keepdims=True) + RMS_EPS) * nw_ref[:, gs]
        o_ref[0, :, gs] = y.astype(o_ref.dtype)


def ssd_core(xbc, z, dt, conv_w, conv_b, dt_bias, a_log, d_skip, norm_w):
    bsz, s, ch = xbc.shape
    di = z.shape[-1]
    nh = di // SSD_HEADDIM
    groups = (ch - di) // (2 * SSD_DSTATE)
    gw = di // groups
    tc = min(SSD_CHUNK, s)
    pad_h = lambda v: jnp.pad(v, (0, LANE - nh)).reshape(1, LANE)
    ltri = jnp.asarray(np.tril(np.ones((tc, tc), np.float32)), BF16)
    expand = np.zeros((LANE, di), np.float32)
    expand[np.arange(di) // SSD_HEADDIM, np.arange(di)] = 1.0
    full = lambda shape: pl.BlockSpec(shape, lambda b, c: (0,) * len(shape))
    return pl.pallas_call(
        functools.partial(_ssd_kernel, tc=tc, di=di, groups=groups),
        grid=(bsz, s // tc),
        in_specs=[
            pl.BlockSpec((1, tc, ch), lambda b, c: (b, c, 0)),
            pl.BlockSpec((1, tc, di), lambda b, c: (b, c, 0)),
            pl.BlockSpec((1, tc, LANE), lambda b, c: (b, c, 0)),
            full((SSD_CONV, ch)), full((1, ch)), full((1, LANE)), full((1, LANE)),
            full((1, di)), full((1, di)), full((tc, tc)), full((LANE, di)),
        ],
        out_specs=pl.BlockSpec((1, tc, di), lambda b, c: (b, c, 0)),
        out_shape=jax.ShapeDtypeStruct((bsz, s, di), BF16),
        scratch_shapes=[
            pltpu.VMEM((groups * SSD_DSTATE, gw), F32),
            pltpu.VMEM((tc + 8, ch), F32),
            pltpu.VMEM((tc, ch), F32),
        ],
        compiler_params=_cparams(("parallel", "arbitrary")),
        name="ssd_core",
    )(xbc, z, dt, conv_w, conv_b.reshape(1, ch), pad_h(dt_bias), pad_h(a_log),
      jnp.repeat(d_skip, SSD_HEADDIM).reshape(1, di), norm_w.reshape(1, di), ltri,
      jnp.asarray(expand, BF16))


def mamba2_layer(x, mod, w_in, conv_w, conv_b, dt_bias, a_log, d_skip, norm_w, w_out,
                 ln_w, ln_b, alpha):
    di = w_out.shape[0]
    ch = conv_w.shape[1]
    wb = w_in.astype(BF16)
    z = mm_mod(x, mod, wb[:, :di], shift_row=0)
    xbc = mm_mod(x, mod, wb[:, di:di + ch], shift_row=0)
    dt = mm_mod(x, mod, _pad_cols(wb[:, di + ch:], LANE), shift_row=0)
    y = ssd_core(xbc, z, dt, conv_w, conv_b, dt_bias, a_log, d_skip, norm_w)
    return mm_res_ln(y, w_out.astype(BF16), x, mod, 2, ln_w, ln_b, alpha)


def _rows(start, size, stride):
    return pl.ds(start, size) if stride == 1 else pl.ds(start, size, stride=stride)


def _da_kernel(*refs, tile, n_heads, scale):
    n_grp = len(DA_PATTERNS)
    ins = refs[:5 * n_grp]
    o_ref, m_ref, l_ref, acc_ref = refs[5 * n_grp:]
    span = DA_SPAN
    first_tile = pl.program_id(2) == 0
    head = jnp.full((1, 2 * span), pl.program_id(1), jnp.int32).astype(F32)
    slope = jnp.exp((head + 1.0) * (-8.0 * math.log(2.0) / n_heads))
    qi = lax.broadcasted_iota(jnp.int32, (span, 2 * span), 0)
    ki = lax.broadcasted_iota(jnp.int32, (span, 2 * span), 1)
    dist = qi + span - ki
    valid = (dist >= 0) & (dist <= span)
    valid_first = valid & jnp.logical_or(ki >= span, jnp.logical_not(first_tile))
    neg = -1e30

    for g, (window, dil) in enumerate(DA_PATTERNS):
        q_ref, kc_ref, vc_ref, kp_ref, vp_ref = ins[5 * g:5 * g + 5]
        bias = dist.astype(F32) * (slope * (-float(dil)))
        for r in range(dil):
            for i in range(tile // (span * dil)):
                qrows = _rows(i * span * dil + r, span, dil)
                q = (q_ref[0, qrows, :] * scale).astype(BF16)
                if i == 0:
                    prow = _rows(r, span, dil)
                    k = jnp.concatenate([kp_ref[0, prow, :], kc_ref[0, prow, :]], axis=0)
                    v = jnp.concatenate([vp_ref[0, prow, :], vc_ref[0, prow, :]], axis=0)
                    ok = valid_first
                else:
                    krows = _rows((i - 1) * span * dil + r, 2 * span, dil)
                    k = kc_ref[0, krows, :]
                    v = vc_ref[0, krows, :]
                    ok = valid
                s = jnp.where(ok, _dot_nt(q, k.astype(BF16)) + bias, neg)
                m_blk = jnp.max(s, axis=-1, keepdims=True)
                if g == 0:
                    m_new = m_blk
                    p = jnp.exp(s - m_new)
                    l_new = jnp.sum(p, axis=-1, keepdims=True)
                    acc_new = _dot(p.astype(BF16), v.astype(BF16))
                else:
                    m_old = m_ref[qrows, :]
                    m_new = jnp.maximum(m_old, m_blk)
                    corr = jnp.exp(m_old - m_new)
                    p = jnp.exp(s - m_new)
                    l_new = corr * l_ref[qrows, :] + jnp.sum(p, axis=-1, keepdims=True)
                    acc_new = corr * acc_ref[qrows, :] + _dot(p.astype(BF16), v.astype(BF16))
                m_ref[qrows, :] = m_new
                l_ref[qrows, :] = l_new
                acc_ref[qrows, :] = acc_new

    o_ref[0] = (acc_ref[...] / l_ref[...]).astype(o_ref.dtype)


def dilated_attention_core(qkv, n_heads):
    bsz, s, _ = qkv.shape
    hd = DA_HEADDIM
    tile = max(w for w, _ in DA_PATTERNS)
    assert s % tile == 0 and all(w // d == DA_SPAN for w, d in DA_PATTERNS)
    in_specs, args = [], []
    for g, (window, dil) in enumerate(DA_PATTERNS):
        prev_rows = DA_SPAN * dil
        per_tile = tile // prev_rows
        for part in (0, 1, 2):
            colb = (g * 3 + part) * n_heads
            in_specs.append(pl.BlockSpec((1, tile, hd), lambda b, h, t, colb=colb: (b, t, colb + h)))
            args.append(qkv)
            if part == 0:
                continue
        for part in (1, 2):
            colb = (g * 3 + part) * n_heads
            in_specs.append(pl.BlockSpec(
                (1, prev_rows, hd),
                lambda b, h, t, colb=colb, per_tile=per_tile: (b, jnp.maximum(t * per_tile - 1, 0), colb + h)))
            args.append(qkv)
    return pl.pallas_call(
        functools.partial(_da_kernel, tile=tile, n_heads=n_heads, scale=hd ** -0.5),
        grid=(bsz, n_heads, s // tile),
        in_specs=in_specs,
        out_specs=pl.BlockSpec((1, tile, hd), lambda b, h, t: (b, t, h)),
        out_shape=jax.ShapeDtypeStruct((bsz, s, n_heads * hd), BF16),
        scratch_shapes=[
            pltpu.VMEM((tile, 1), F32),
            pltpu.VMEM((tile, 1), F32),
            pltpu.VMEM((tile, hd), F32),
        ],
        compiler_params=_cparams(("parallel", "parallel", "arbitrary")),
        name="dilated_attention",
    )(*args)


def dilated_attention_layer(x, mod, w_in, w_out, ln_w, ln_b, alpha):
    d = x.shape[-1]
    qkv = mm_mod(x, mod, w_in.astype(BF16), shift_row=0)
    o = dilated_attention_core(qkv, d // DA_HEADDIM)
    return mm_res_ln(o, w_out.astype(BF16), x, mod, 2, ln_w, ln_b, alpha)


RW_TN = 256


def _rw_mix_kernel(x_ref, xp_ref, mod_ref, mu_ref, w_ref, o_ref, h_scr, xx_scr):
    @pl.when(pl.program_id(2) == 0)
    def _():
        shift = mod_ref[0, 0:1, :]
        scale = 1.0 + mod_ref[0, 1:2, :]
        h = x_ref[0] * scale + shift
        prev_row = xp_ref[0, 7:8, :] * scale + shift
        prev_row = jnp.where(pl.program_id(1) == 0, 0.0, prev_row)
        rolled = pltpu.roll(h, 1, 0)
        row = lax.broadcasted_iota(jnp.int32, h.shape, 0)
        h_scr[...] = h
        xx_scr[...] = jnp.where(row == 0, prev_row, rolled) - h

    mixed = (h_scr[...] + xx_scr[...] * mu_ref[0]).astype(BF16)
    o_ref[0] = _dot(mixed, w_ref[...])


def rw_mix_mm(x, mod, mu_tiles, w_all):
    bsz, s, d = x.shape
    n = w_all.shape[1]
    tm = _pick(s, (512, 256, 128))
    return pl.pallas_call(
        _rw_mix_kernel,
        grid=(bsz, s // tm, n // RW_TN),
        in_specs=[
            pl.BlockSpec((1, tm, d), lambda b, i, j: (b, i, 0)),
            pl.BlockSpec((1, 8, d), lambda b, i, j: (b, jnp.maximum(i * (tm // 8) - 1, 0), 0)),
            pl.BlockSpec((1, N_MOD, d), lambda b, i, j: (b, 0, 0)),
            pl.BlockSpec((1, 1, d), lambda b, i, j: (j, 0, 0)),
            pl.BlockSpec((d, RW_TN), lambda b, i, j: (0, j)),
        ],
        out_specs=pl.BlockSpec((1, tm, RW_TN), lambda b, i, j: (b, i, j)),
        out_shape=jax.ShapeDtypeStruct((bsz, s, n), F32),
        scratch_shapes=[pltpu.VMEM((tm, d), F32), pltpu.VMEM((tm, d), F32)],
        compiler_params=_cparams(("parallel", "parallel", "arbitrary")),
        name="rw_mix_mm",
    )(x, x, mod, mu_tiles, w_all)


def _rw_lora2_kernel(w1h_ref, a1h_ref, g1h_ref, w2_ref, a2_ref, g2_ref, w0_ref, a0_ref,
                     wp_ref, ap_ref, gt_ref):
    wp_ref[0] = w0_ref[...] + _dot(jnp.tanh(w1h_ref[0]).astype(BF16), w2_ref[...])
    ap_ref[0] = a0_ref[...] + _dot(a1h_ref[0].astype(BF16), a2_ref[...])
    gt_ref[0] = _dot(_sigmoid(g1h_ref[0]).astype(BF16), g2_ref[...])


def rw_lora2(proj, col0, w2, a2, g2, w0, a0):
    bsz, s, _ = proj.shape
    d = w2.shape[1]
    tm = _pick(s, (256, 128))
    cb = col0 // RW_TN
    hspec = lambda k: pl.BlockSpec((1, tm, RW_TN), lambda b, i, k=k: (b, i, cb + k))
    wspec = pl.BlockSpec((RW_TN, d), lambda b, i: (0, 0))
    vspec = pl.BlockSpec((1, d), lambda b, i: (0, 0))
    ospec = pl.BlockSpec((1, tm, d), lambda b, i: (b, i, 0))
    oshape = jax.ShapeDtypeStruct((bsz, s, d), F32)
    return pl.pallas_call(
        _rw_lora2_kernel,
        grid=(bsz, s // tm),
        in_specs=[hspec(0), hspec(1), hspec(2), wspec, wspec, wspec, vspec, vspec],
        out_specs=[ospec, ospec, ospec],
        out_shape=[oshape, oshape, oshape],
        compiler_params=_cparams(("parallel", "parallel")),
        name="rw_lora2",
    )(proj, proj, proj, w2, a2, g2, w0.reshape(1, d), a0.reshape(1, d))


def _rw_scan_kernel(r_ref, k_ref, v_ref, wp_ref, ap_ref, kk_ref, ka_ref, rk_ref, lnw_ref, lnb_ref,
                    o_ref, st_ref, dec_ref, kkn_ref, bb_ref, kh_ref, *, tb, lw):
    hd = RW_HEADDIM
    bh = r_ref.shape[-1]

    @pl.when(pl.program_id(0) == 0)
    def _():
        st_ref[...] = jnp.zeros_like(st_ref)

    def prep(t, carry):
        a = _sigmoid(ap_ref[t])
        kraw = k_ref[t]
        kk = kraw * kk_ref[...]
        nrm = jnp.sum(kk * kk, axis=0, keepdims=True)
        kkn = kk * lax.rsqrt(jnp.maximum(nrm, 1e-24))
        dec_ref[t] = jnp.exp(-jnp.exp(-_softplus(-wp_ref[t]) - 0.5))
        kkn_ref[t] = kkn
        bb_ref[t] = kkn * a
        kh_ref[t] = kraw * (1.0 + (a - 1.0) * ka_ref[...])
        return carry

    lax.fori_loop(0, tb, prep, 0)

    sub = 8

    def rows8(ref, t, kb, lanes):
        return ref[t, pl.ds(pl.multiple_of(kb * sub, sub), sub), lanes]

    def bcast(tile, j):
        return jnp.broadcast_to(tile[j:j + 1, :], (hd, lw))

    def step(t, carry):
        for c in range(bh // lw):
            lanes = slice(c * lw, (c + 1) * lw)

            def pass1(kb, sa):
                a8 = rows8(kkn_ref, t, kb, lanes)
                for j in range(sub):
                    sa = sa + st_ref[kb * sub + j, :, lanes] * bcast(a8, j)
                return sa

            sa = -lax.fori_loop(0, hd // sub, pass1, jnp.zeros((hd, lw), F32))
            vt = v_ref[t, :, lanes]

            def pass2(kb, y):
                d8 = rows8(dec_ref, t, kb, lanes)
                b8 = rows8(bb_ref, t, kb, lanes)
                k8 = rows8(kh_ref, t, kb, lanes)
                r8 = rows8(r_ref, t, kb, lanes)
                for j in range(sub):
                    s_new = (st_ref[kb * sub + j, :, lanes] * bcast(d8, j)
                             + sa * bcast(b8, j) + vt * bcast(k8, j))
                    st_ref[kb * sub + j, :, lanes] = s_new
                    y = y + s_new * bcast(r8, j)
                return y

            y = lax.fori_loop(0, hd // sub, pass2, jnp.zeros((hd, lw), F32))
            mu = jnp.mean(y, axis=0, keepdims=True)
            yc = y - mu
            var = jnp.mean(yc * yc, axis=0, keepdims=True)
            yn = yc * lax.rsqrt(var + RW_GN_EPS) * lnw_ref[:, lanes] + lnb_ref[:, lanes]
            bonus = jnp.sum(r_ref[t, :, lanes] * kh_ref[t, :, lanes] * rk_ref[:, lanes],
                            axis=0, keepdims=True)
            o_ref[t, :, lanes] = yn + bonus * vt
        return carry

    lax.fori_loop(0, tb, step, 0)


def rw_scan(r, k, v, wpre, apre, k_k, k_a, r_k, ln_w, ln_b):
    s, hd, bh = r.shape
    tb = _pick(s, (16, 8))
    lw = min(LANE, bh)
    seq = pl.BlockSpec((tb, hd, bh), lambda i: (i, 0, 0))
    par = pl.BlockSpec((hd, bh), lambda i: (0, 0))
    return pl.pallas_call(
        functools.partial(_rw_scan_kernel, tb=tb, lw=lw),
        grid=(s // tb,),
        in_specs=[seq] * 5 + [par] * 5,
        out_specs=seq,
        out_shape=jax.ShapeDtypeStruct((s, hd, bh), F32),
        scratch_shapes=[pltpu.VMEM((hd, hd, bh), F32)] + [pltpu.VMEM((tb, hd, bh), F32)] * 4,
        compiler_params=_cparams(("arbitrary",)),
        name="rw_scan",
    )(r, k, v, wpre, apre, k_k, k_a, r_k, ln_w, ln_b)


def rwkv7_layer(x, mod, mu, w_rkv, w0, w1, w2, a0, a1, a2, g1, g2, k_k, k_a, r_k, gn_w, gn_b,
                w_out, ln_w, ln_b, alpha):
    bsz, s, d = x.shape
    nh = d // RW_HEADDIM
    pad_c = lambda w: _pad_cols(w, RW_TN)
    pad_r = lambda w: jnp.pad(w, ((0, RW_TN - w.shape[0]), (0, 0)))
    secs = [w_rkv[0], w_rkv[1], w_rkv[2], pad_c(w1), pad_c(a1), pad_c(g1)]
    w_all = jnp.concatenate(secs, axis=1).astype(BF16)
    mu_tiles = jnp.concatenate(
        [jnp.broadcast_to(mu[i], (secs[i].shape[1] // RW_TN, d)) for i in (0, 1, 2, 3, 4, 5)],
        axis=0)[:, None, :]
    proj = rw_mix_mm(x, mod, mu_tiles, w_all)
    wpre, apre, gate = rw_lora2(proj, 3 * d, pad_r(w2).astype(BF16), pad_r(a2).astype(BF16),
                                pad_r(g2).astype(BF16), w0, a0)

    def to_scan(t):
        return t.reshape(bsz, s, nh, RW_HEADDIM).transpose(1, 3, 0, 2).reshape(s, RW_HEADDIM, bsz * nh)

    def par(p):
        return jnp.tile(p.reshape(nh, RW_HEADDIM).T[:, None, :], (1, bsz, 1)).reshape(RW_HEADDIM, bsz * nh)

    y = rw_scan(to_scan(proj[..., 0:d]), to_scan(proj[..., d:2 * d]), to_scan(proj[..., 2 * d:3 * d]),
                to_scan(wpre), to_scan(apre), par(k_k), par(k_a), par(r_k.reshape(-1)),
                par(gn_w), par(gn_b))
    y = y.reshape(s, RW_HEADDIM, bsz, nh).transpose(2, 0, 3, 1).reshape(bsz, s, d)
    return mm_res_ln(y, w_out.astype(BF16), x, mod, 2, ln_w, ln_b, alpha, h2=gate)


def kernel(x, c, ada_w, ada_b, ln_w, ln_b, hg_w_in, hg_lb_logits, hg_norm_w, hg_w_out, ssd_w_in, ssd_conv_w, ssd_conv_b, ssd_dt_bias, ssd_a_log, ssd_d, ssd_norm_w, ssd_w_out, da_w_in, da_w_out, rw_mu, rw_w_rkv, rw_w0, rw_w1, rw_w2, rw_a0, rw_a1, rw_a2, rw_g1, rw_g2, rw_k_k, rw_k_a, rw_r_k, rw_ln_w, rw_ln_b, rw_w_out, ffn_w_gu, ffn_w_down, moe_router, moe_w_gu, moe_w_down):
    depth = ada_w.shape[0]
    bsz, _, d = x.shape
    alpha = (2.0 * depth) ** 0.25
    lbs = lower_bounds(hg_lb_logits)
    mods = ada_modulation(c, ada_w, ada_b).reshape(depth, bsz, N_MOD, d)
    for i in range(depth):
        mixer, j = i % 4, i // 4
        mod = mods[i]
        if mixer == 0:
            x = hgrn2_layer(x, mod, hg_w_in[j], lbs[i], hg_norm_w[j], hg_w_out[j],
                            ln_w[i, 0], ln_b[i, 0], alpha)
        elif mixer == 1:
            x = mamba2_layer(x, mod, ssd_w_in[j], ssd_conv_w[j], ssd_conv_b[j], ssd_dt_bias[j],
                             ssd_a_log[j], ssd_d[j], ssd_norm_w[j], ssd_w_out[j],
                             ln_w[i, 0], ln_b[i, 0], alpha)
        elif mixer == 2:
            x = dilated_attention_layer(x, mod, da_w_in[j], da_w_out[j], ln_w[i, 0], ln_b[i, 0], alpha)
        else:
            x = rwkv7_layer(x, mod, rw_mu[j], rw_w_rkv[j], rw_w0[j], rw_w1[j], rw_w2[j], rw_a0[j],
                            rw_a1[j], rw_a2[j], rw_g1[j], rw_g2[j], rw_k_k[j], rw_k_a[j], rw_r_k[j],
                            rw_ln_w[j], rw_ln_b[j], rw_w_out[j], ln_w[i, 0], ln_b[i, 0], alpha)
        if i % 2 == 0:
            x = dense_ffn(x, mod, ffn_w_gu[i // 2], ffn_w_down[i // 2], ln_w[i, 1], ln_b[i, 1], alpha)
        else:
            x = moe_ffn(x, mod, moe_router[i // 2], moe_w_gu[i // 2], moe_w_down[i // 2],
                        ln_w[i, 1], ln_b[i, 1], alpha)
    return x
```

```python
import functools
import math

import numpy as np
import jax
import jax.numpy as jnp
from jax import lax
from jax.experimental import pallas as pl
from jax.experimental.pallas import tpu as pltpu

F32 = jnp.float32
BF16 = jnp.bfloat16

N_MOD = 6
LN_EPS = 1e-5
RMS_EPS = 1e-6

HG_DK = 128
GLA_CHUNK = 64
SSD_HEADDIM = 64
SSD_DSTATE = 128
SSD_CONV = 4
SSD_CHUNK = 256
DA_PATTERNS = ((128, 1), (512, 4), (2048, 16))
DA_HEADDIM = 128
DA_SPAN = 128
RW_HEADDIM = 64
RW_GN_EPS = 64e-5
MOE_TOPK = 2

LANE = 128
VMEM_LIMIT = 56 * 1024 * 1024


def _pick(n, cands):
    for c in cands:
        if n % c == 0:
            return c
    return n


def _cparams(sem):
    return pltpu.CompilerParams(dimension_semantics=sem, vmem_limit_bytes=VMEM_LIMIT)


def _sigmoid(x):
    return 1.0 / (1.0 + jnp.exp(-x))


def _silu(x):
    return x * _sigmoid(x)


def _softplus(x):
    return jnp.maximum(x, 0.0) + jnp.log(1.0 + jnp.exp(-jnp.abs(x)))


def _dot(a, b):
    return jnp.dot(a, b, preferred_element_type=F32)


def _dot_nt(a, b):
    return lax.dot_general(a, b, (((1,), (1,)), ((), ())), preferred_element_type=F32)


def _dot_tn(a, b):
    return lax.dot_general(a, b, (((0,), (0,)), ((), ())), preferred_element_type=F32)


def _ada_kernel(c_ref, w_ref, b_ref, o_ref):
    o_ref[0] = _dot(c_ref[...].astype(BF16), w_ref[0].astype(BF16)) + b_ref[0]


def ada_modulation(c, ada_w, ada_b):
    depth, d, n = ada_w.shape
    bsz = c.shape[0]
    tn = _pick(n, (1024, 512, 256, 128))
    return pl.pallas_call(
        _ada_kernel,
        grid=(depth, n // tn),
        in_specs=[
            pl.BlockSpec((bsz, d), lambda i, j: (0, 0)),
            pl.BlockSpec((1, d, tn), lambda i, j: (i, 0, j)),
            pl.BlockSpec((1, 1, tn), lambda i, j: (i, 0, j)),
        ],
        out_specs=pl.BlockSpec((1, bsz, tn), lambda i, j: (i, 0, j)),
        out_shape=jax.ShapeDtypeStruct((depth, bsz, n), F32),
        compiler_params=_cparams(("parallel", "parallel")),
        name="ada_modulation",
    )(c, ada_w, ada_b.reshape(depth, 1, n))


def _mm_mod_kernel(x_ref, mod_ref, w_ref, o_ref, h_scr, *, shift_row):
    @pl.when(pl.program_id(2) == 0)
    def _():
        shift = mod_ref[0, shift_row:shift_row + 1, :]
        scale = mod_ref[0, shift_row + 1:shift_row + 2, :]
        h_scr[...] = (x_ref[0] * (1.0 + scale) + shift).astype(BF16)

    o_ref[0] = _dot(h_scr[...], w_ref[...]).astype(o_ref.dtype)


def mm_mod(x, mod, w, shift_row, out_dtype=F32):
    bsz, s, d = x.shape
    n = w.shape[1]
    tm = _pick(s, (1024, 512, 256, 128))
    tn = _pick(n, (512, 256, 128))
    return pl.pallas_call(
        functools.partial(_mm_mod_kernel, shift_row=shift_row),
        grid=(bsz, s // tm, n // tn),
        in_specs=[
            pl.BlockSpec((1, tm, d), lambda b, i, j: (b, i, 0)),
            pl.BlockSpec((1, N_MOD, d), lambda b, i, j: (b, 0, 0)),
            pl.BlockSpec((d, tn), lambda b, i, j: (0, j)),
        ],
        out_specs=pl.BlockSpec((1, tm, tn), lambda b, i, j: (b, i, j)),
        out_shape=jax.ShapeDtypeStruct((bsz, s, n), out_dtype),
        scratch_shapes=[pltpu.VMEM((tm, d), BF16)],
        compiler_params=_cparams(("parallel", "parallel", "arbitrary")),
        name="mm_mod",
    )(x, mod, w)


def _mm_res_ln_kernel(*refs, gate_row, has_h2, alpha):
    if has_h2:
        h_ref, h2_ref, w_ref, x_ref, mod_ref, lnw_ref, lnb_ref, o_ref = refs
        h = (h_ref[0] * h2_ref[0]).astype(BF16)
    else:
        h_ref, w_ref, x_ref, mod_ref, lnw_ref, lnb_ref, o_ref = refs
        h = h_ref[0].astype(BF16)
    gate = mod_ref[0, gate_row:gate_row + 1, :]
    z = alpha * x_ref[0] + (1.0 + gate) * _dot(h, w_ref[...])
    mu = jnp.mean(z, axis=-1, keepdims=True)
    zc = z - mu
    var = jnp.mean(zc * zc, axis=-1, keepdims=True)
    o_ref[0] = zc * lax.rsqrt(var + LN_EPS) * lnw_ref[...] + lnb_ref[...]


def mm_res_ln(h, w, x, mod, gate_row, ln_w, ln_b, alpha, h2=None):
    bsz, s, d = x.shape
    kdim = h.shape[-1]
    tm = _pick(s, (512, 256, 128) if kdim <= 2 * d else (256, 128))
    h_spec = pl.BlockSpec((1, tm, kdim), lambda b, i: (b, i, 0))
    ins = [h] + ([h2] if h2 is not None else [])
    in_specs = [h_spec] * len(ins) + [
        pl.BlockSpec((kdim, d), lambda b, i: (0, 0), pipeline_mode=pl.Buffered(1)),
        pl.BlockSpec((1, tm, d), lambda b, i: (b, i, 0)),
        pl.BlockSpec((1, N_MOD, d), lambda b, i: (b, 0, 0)),
        pl.BlockSpec((1, d), lambda b, i: (0, 0)),
        pl.BlockSpec((1, d), lambda b, i: (0, 0)),
    ]
    return pl.pallas_call(
        functools.partial(_mm_res_ln_kernel, gate_row=gate_row, has_h2=h2 is not None, alpha=alpha),
        grid=(bsz, s // tm),
        in_specs=in_specs,
        out_specs=pl.BlockSpec((1, tm, d), lambda b, i: (b, i, 0)),
        out_shape=jax.ShapeDtypeStruct((bsz, s, d), F32),
        compiler_params=_cparams(("parallel", "parallel")),
        name="mm_res_ln",
    )(*ins, w, x, mod, ln_w.reshape(1, d), ln_b.reshape(1, d))


def _mm_swiglu_kernel(x_ref, mod_ref, wg_ref, wu_ref, o_ref, h_scr, *, shift_row):
    @pl.when(pl.program_id(2) == 0)
    def _():
        shift = mod_ref[0, shift_row:shift_row + 1, :]
        scale = mod_ref[0, shift_row + 1:shift_row + 2, :]
        h_scr[...] = (x_ref[0] * (1.0 + scale) + shift).astype(BF16)

    h = h_scr[...]
    g = _dot(h, wg_ref[...])
    u = _dot(h, wu_ref[...])
    o_ref[0] = (_silu(g) * u).astype(o_ref.dtype)


def mm_swiglu(x, mod, wg, wu, shift_row):
    bsz, s, d = x.shape
    f = wg.shape[1]
    tm = _pick(s, (1024, 512, 256, 128))
    tn = _pick(f, (512, 256, 128))
    return pl.pallas_call(
        functools.partial(_mm_swiglu_kernel, shift_row=shift_row),
        grid=(bsz, s // tm, f // tn),
        in_specs=[
            pl.BlockSpec((1, tm, d), lambda b, i, j: (b, i, 0)),
            pl.BlockSpec((1, N_MOD, d), lambda b, i, j: (b, 0, 0)),
            pl.BlockSpec((d, tn), lambda b, i, j: (0, j)),
            pl.BlockSpec((d, tn), lambda b, i, j: (0, j)),
        ],
        out_specs=pl.BlockSpec((1, tm, tn), lambda b, i, j: (b, i, j)),
        out_shape=jax.ShapeDtypeStruct((bsz, s, f), BF16),
        scratch_shapes=[pltpu.VMEM((tm, d), BF16)],
        compiler_params=_cparams(("parallel", "parallel", "arbitrary")),
        name="mm_swiglu",
    )(x, mod, wg, wu)


def _pad_cols(w, mult):
    n = w.shape[-1]
    pad = (-n) % mult
    return jnp.pad(w, ((0, 0), (0, pad))) if pad else w


def dense_ffn(x, mod, w_gu, w_down, ln_w, ln_b, alpha):
    f = w_gu.shape[1] // 2
    fmult = 512 if f >= 512 else LANE
    wg = _pad_cols(w_gu[:, :f], fmult).astype(BF16)
    wu = _pad_cols(w_gu[:, f:], fmult).astype(BF16)
    wd = jnp.pad(w_down, ((0, wg.shape[1] - f), (0, 0))).astype(BF16)
    hmid = mm_swiglu(x, mod, wg, wu, shift_row=3)
    return mm_res_ln(hmid, wd, x, mod, 5, ln_w, ln_b, alpha)


MOE_TM = 512
SEL_I1, SEL_I2, SEL_G1, SEL_G2, SEL_R1, SEL_R2 = range(6)


def _route_kernel(x_ref, mod_ref, wr_ref, ltri_ref, sel_ref, hm_ref, cnt_ref, cnt_scr, *, n_experts):
    @pl.when((pl.program_id(0) == 0) & (pl.program_id(1) == 0))
    def _():
        cnt_scr[...] = jnp.zeros_like(cnt_scr)

    shift = mod_ref[0, 3:4, :]
    scale = mod_ref[0, 4:5, :]
    h = x_ref[0] * (1.0 + scale) + shift
    hm_ref[0] = h
    logits = jnp.dot(h, wr_ref[...], preferred_element_type=F32, precision=lax.Precision.HIGHEST)
    lane = lax.broadcasted_iota(jnp.int32, logits.shape, 1)
    ninf = -jnp.inf
    lg = jnp.where(lane < n_experts, logits, ninf)
    m1 = jnp.max(lg, axis=-1, keepdims=True)
    i1 = jnp.min(jnp.where(lg == m1, lane, LANE), axis=-1, keepdims=True)
    lg2 = jnp.where(lane == i1, ninf, lg)
    m2 = jnp.max(lg2, axis=-1, keepdims=True)
    i2 = jnp.min(jnp.where(lg2 == m2, lane, LANE), axis=-1, keepdims=True)
    e2 = jnp.exp(m2 - m1)
    g1 = 1.0 / (1.0 + e2)
    hot1 = lane == i1
    hot2 = lane == i2
    chosen = jnp.where(hot1 | hot2, 1.0, 0.0)
    before = _dot(ltri_ref[...], chosen.astype(BF16)) + cnt_scr[...]
    r1 = jnp.sum(jnp.where(hot1, before, 0.0), axis=-1, keepdims=True)
    r2 = jnp.sum(jnp.where(hot2, before, 0.0), axis=-1, keepdims=True)
    cnt_scr[...] += jnp.sum(chosen, axis=0, keepdims=True)
    rec = jnp.zeros_like(logits)
    for slot, val in ((SEL_I1, i1.astype(F32)), (SEL_I2, i2.astype(F32)), (SEL_G1, g1),
                      (SEL_G2, e2 * g1), (SEL_R1, r1), (SEL_R2, r2)):
        rec = jnp.where(lane == slot, val, rec)
    sel_ref[0] = rec
    cnt_ref[...] = jnp.broadcast_to(cnt_scr[...], cnt_ref.shape)


def moe_route(x, mod, w_router):
    bsz, s, d = x.shape
    n_experts = w_router.shape[1]
    tm = _pick(s, (256, 128))
    ltri = jnp.asarray(np.tril(np.ones((tm, tm), np.float32), -1), BF16)
    return pl.pallas_call(
        functools.partial(_route_kernel, n_experts=n_experts),
        grid=(bsz, s // tm),
        in_specs=[
            pl.BlockSpec((1, tm, d), lambda b, i: (b, i, 0)),
            pl.BlockSpec((1, N_MOD, d), lambda b, i: (b, 0, 0)),
            pl.BlockSpec((d, LANE), lambda b, i: (0, 0)),
            pl.BlockSpec((tm, tm), lambda b, i: (0, 0)),
        ],
        out_specs=[
            pl.BlockSpec((1, tm, LANE), lambda b, i: (b, i, 0)),
            pl.BlockSpec((1, tm, d), lambda b, i: (b, i, 0)),
            pl.BlockSpec((8, LANE), lambda b, i: (0, 0)),
        ],
        out_shape=[
            jax.ShapeDtypeStruct((bsz, s, LANE), F32),
            jax.ShapeDtypeStruct((bsz, s, d), F32),
            jax.ShapeDtypeStruct((8, LANE), F32),
        ],
        scratch_shapes=[pltpu.VMEM((1, LANE), F32)],
        compiler_params=_cparams(("arbitrary", "arbitrary")),
        name="moe_route",
    )(x, mod, _pad_cols(w_router, LANE), ltri)


def _row_copy(src_ref, src_row, dst_ref, dst_row, sem):
    return pltpu.make_async_copy(src_ref.at[pl.ds(src_row, 1)], dst_ref.at[pl.ds(dst_row, 1)], sem)


def _dispatch_kernel(pos_ref, hm_ref, zero_ref, hs_ref, sem, *, tm):
    del zero_ref
    src = hm_ref.at[0]

    def start(r, c):
        for k in range(MOE_TOPK):
            _row_copy(src, r, hs_ref, pos_ref[0, 0, MOE_TOPK * r + k], sem).start()
        return c

    def wait(r, c):
        for k in range(MOE_TOPK):
            _row_copy(src, r, hs_ref, pos_ref[0, 0, MOE_TOPK * r + k], sem).wait()
        return c

    lax.fori_loop(0, tm, start, 0)
    lax.fori_loop(0, tm, wait, 0)


def moe_dispatch(hm, pos, n_rows):
    bsz, s, d = hm.shape
    tm = _pick(s, (256, 128))
    nt = s // tm
    pos_t = pos.reshape(bsz * nt, 1, MOE_TOPK * tm)
    return pl.pallas_call(
        functools.partial(_dispatch_kernel, tm=tm),
        grid=(bsz, nt),
        in_specs=[
            pl.BlockSpec((1, 1, MOE_TOPK * tm), lambda b, i: (b * nt + i, 0, 0), memory_space=pltpu.SMEM),
            pl.BlockSpec((1, tm, d), lambda b, i: (b, i, 0)),
            pl.BlockSpec(memory_space=pl.ANY),
        ],
        out_specs=pl.BlockSpec(memory_space=pl.ANY),
        out_shape=jax.ShapeDtypeStruct((n_rows, d), F32),
        scratch_shapes=[pltpu.SemaphoreType.DMA(())],
        input_output_aliases={2: 0},
        compiler_params=_cparams(("arbitrary", "arbitrary")),
        name="moe_dispatch",
    )(pos_t, hm, jnp.zeros((n_rows, d), F32))


def _moe_gu_kernel(te_ref, nu_ref, hs_ref, wg_ref, wu_ref, o_ref, h_scr):
    del te_ref

    @pl.when(pl.program_id(0) < nu_ref[0])
    def _():
        @pl.when(pl.program_id(1) == 0)
        def _():
            h_scr[...] = hs_ref[...].astype(BF16)

        h = h_scr[...]
        g = _dot(h, wg_ref[0])
        u = _dot(h, wu_ref[0])
        o_ref[...] = (_silu(g) * u).astype(o_ref.dtype)

    @pl.when(pl.program_id(0) >= nu_ref[0])
    def _():
        o_ref[...] = jnp.zeros_like(o_ref)


def _moe_down_kernel(te_ref, nu_ref, h_ref, w_ref, o_ref):
    del te_ref

    @pl.when(pl.program_id(1) < nu_ref[0])
    def _():
        o_ref[...] = _dot(h_ref[...], w_ref[0])

    @pl.when(pl.program_id(1) >= nu_ref[0])
    def _():
        o_ref[...] = jnp.zeros_like(o_ref)


def _moe_combine_kernel(pos_ref, ys_ref, sel_ref, x_ref, mod_ref, lnw_ref, lnb_ref, o_ref, buf, sem, *,
                        tm, alpha):
    def start(r, c):
        for k in range(MOE_TOPK):
            _row_copy(ys_ref, pos_ref[0, 0, MOE_TOPK * r + k], buf.at[k], r, sem).start()
        return c

    def wait(r, c):
        for k in range(MOE_TOPK):
            _row_copy(ys_ref, pos_ref[0, 0, MOE_TOPK * r + k], buf.at[k], r, sem).wait()
        return c

    lax.fori_loop(0, tm, start, 0)
    lax.fori_loop(0, tm, wait, 0)
    sel = sel_ref[0]
    lane = lax.broadcasted_iota(jnp.int32, sel.shape, 1)
    g1 = jnp.sum(jnp.where(lane == SEL_G1, sel, 0.0), axis=-1, keepdims=True)
    g2 = jnp.sum(jnp.where(lane == SEL_G2, sel, 0.0), axis=-1, keepdims=True)
    y = g1 * buf[0] + g2 * buf[1]
    z = alpha * x_ref[0] + (1.0 + mod_ref[0, 5:6, :]) * y
    mu = jnp.mean(z, axis=-1, keepdims=True)
    zc = z - mu
    var = jnp.mean(zc * zc, axis=-1, keepdims=True)
    o_ref[0] = zc * lax.rsqrt(var + LN_EPS) * lnw_ref[...] + lnb_ref[...]


def moe_ffn(x, mod, w_router, w_gu, w_down, ln_w, ln_b, alpha):
    bsz, s, d = x.shape
    n_experts, _, ff2 = w_gu.shape
    ff = ff2 // 2
    tmr = min(MOE_TM, s)
    n_rows = MOE_TOPK * bsz * s + n_experts * tmr
    n_tiles = n_rows // tmr

    sel, hm, cnt = moe_route(x, mod, w_router)

    counts = cnt[0, :n_experts].astype(jnp.int32)
    padded = (counts + tmr - 1) // tmr * tmr
    ends = jnp.cumsum(padded)
    offs = ends - padded
    tile_expert = jnp.minimum(
        jnp.searchsorted(ends, jnp.arange(n_tiles, dtype=jnp.int32) * tmr, side="right"),
        n_experts - 1).astype(jnp.int32)
    n_used = (ends[-1:] // tmr).astype(jnp.int32)
    ids = sel[..., SEL_I1:SEL_I2 + 1].astype(jnp.int32)
    ranks = sel[..., SEL_R1:SEL_R2 + 1].astype(jnp.int32)
    hot = ids[..., None] == jnp.arange(n_experts, dtype=jnp.int32)
    pos = jnp.sum(jnp.where(hot, offs, 0), axis=-1) + ranks

    hs = moe_dispatch(hm, pos, n_rows)

    tn = _pick(ff, (512, 256, 128))
    nt = ff // tn
    wb = w_gu.astype(BF16)
    live = lambda i, nu: jnp.minimum(i, nu[0] - 1)
    col = lambda i, j, nu: jnp.where(i < nu[0], j, nt - 1)
    hmid = pl.pallas_call(
        _moe_gu_kernel,
        grid_spec=pltpu.PrefetchScalarGridSpec(
            num_scalar_prefetch=2,
            grid=(n_tiles, nt),
            in_specs=[
                pl.BlockSpec((tmr, d), lambda i, j, te, nu: (live(i, nu), 0)),
                pl.BlockSpec((1, d, tn), lambda i, j, te, nu: (te[live(i, nu)], 0, col(i, j, nu))),
                pl.BlockSpec((1, d, tn), lambda i, j, te, nu: (te[live(i, nu)], 0, nt + col(i, j, nu))),
            ],
            out_specs=pl.BlockSpec((tmr, tn), lambda i, j, te, nu: (i, j)),
            scratch_shapes=[pltpu.VMEM((tmr, d), BF16)],
        ),
        out_shape=jax.ShapeDtypeStruct((n_rows, ff), BF16),
        compiler_params=_cparams(("arbitrary", "arbitrary")),
        name="moe_gate_up",
    )(tile_expert, n_used, hs, wb, wb)

    dn = _pick(d, (1024, 512, 256, 128))
    ys = pl.pallas_call(
        _moe_down_kernel,
        grid_spec=pltpu.PrefetchScalarGridSpec(
            num_scalar_prefetch=2,
            grid=(d // dn, n_tiles),
            in_specs=[
                pl.BlockSpec((tmr, ff), lambda n, i, te, nu: (live(i, nu), 0)),
                pl.BlockSpec((1, ff, dn), lambda n, i, te, nu: (te[live(i, nu)], 0, n)),
            ],
            out_specs=pl.BlockSpec((tmr, dn), lambda n, i, te, nu: (i, n)),
        ),
        out_shape=jax.ShapeDtypeStruct((n_rows, d), F32),
        compiler_params=_cparams(("arbitrary", "arbitrary")),
        name="moe_down",
    )(tile_expert, n_used, hmid, w_down.astype(BF16))

    tm = _pick(s, (256, 128))
    ntk = s // tm
    return pl.pallas_call(
        functools.partial(_moe_combine_kernel, tm=tm, alpha=alpha),
        grid=(bsz, ntk),
        in_specs=[
            pl.BlockSpec((1, 1, MOE_TOPK * tm), lambda b, i: (b * ntk + i, 0, 0), memory_space=pltpu.SMEM),
            pl.BlockSpec(memory_space=pl.ANY),
            pl.BlockSpec((1, tm, LANE), lambda b, i: (b, i, 0)),
            pl.BlockSpec((1, tm, d), lambda b, i: (b, i, 0)),
            pl.BlockSpec((1, N_MOD, d), lambda b, i: (b, 0, 0)),
            pl.BlockSpec((1, d), lambda b, i: (0, 0)),
            pl.BlockSpec((1, d), lambda b, i: (0, 0)),
        ],
        out_specs=pl.BlockSpec((1, tm, d), lambda b, i: (b, i, 0)),
        out_shape=jax.ShapeDtypeStruct((bsz, s, d), F32),
        scratch_shapes=[pltpu.VMEM((MOE_TOPK, tm, d), F32), pltpu.SemaphoreType.DMA(())],
        compiler_params=_cparams(("arbitrary", "arbitrary")),
        name="moe_combine",
    )(pos.reshape(bsz * ntk, 1, MOE_TOPK * tm), ys, sel, x, mod, ln_w.reshape(1, d), ln_b.reshape(1, d))


def _gla_tables(chunk):
    t = np.arange(chunk)[:, None]
    u = np.arange(chunk)[None, :]
    mats = [(u <= t), (u > t)]
    levels = int(math.log2(chunk))
    for l in range(1, levels + 1):
        half = 1 << (l - 1)
        anchor = ((t >> l) << l) + half - 1
        upper = (t & half) != 0
        mats.append(np.where(upper, (u > anchor) & (u <= t), (u > t) & (u <= anchor)))
    return np.concatenate(mats, axis=0).astype(np.float32), levels


def _gla_kernel(q_ref, f_ref, i_ref, g_ref, lb_ref, nw_ref, m_ref, o_ref, st_ref, *,
                chunk, levels, n_chunks):
    @pl.when(pl.program_id(2) == 0)
    def _():
        st_ref[...] = jnp.zeros_like(st_ref)

    lb = lb_ref[...]
    nw = nw_ref[...]
    row = lax.broadcasted_iota(jnp.int32, (chunk, HG_DK), 0)
    srow = lax.broadcasted_iota(jnp.int32, (chunk, chunk), 0)
    scol = lax.broadcasted_iota(jnp.int32, (chunk, chunk), 1)

    chunks = range(n_chunks)
    qq, kk, vv, e = [], [], [], []
    for c in chunks:
        rows = slice(c * chunk, (c + 1) * chunk)
        fg = lb + (1.0 - lb) * _sigmoid(f_ref[0, rows, :])
        logf = jnp.log(fg)
        kk.append(1.0 - fg)
        qq.append(_silu(q_ref[0, rows, :]))
        vv.append(i_ref[0, rows, :])
        g_hi = logf.astype(BF16)
        g_lo = (logf - g_hi.astype(F32)).astype(BF16)
        e2 = _dot(m_ref[...], jnp.concatenate([g_hi, g_lo], axis=1))
        e.append(e2[:, :HG_DK] + e2[:, HG_DK:])

    scores = [None] * n_chunks
    for l in range(1, levels + 1):
        half = 1 << (l - 1)
        upper = (row & half) != 0
        same_block = (srow >> l) == (scol >> l)
        for c in chunks:
            a = jnp.exp(e[c][(l + 1) * chunk:(l + 2) * chunk])
            qa = jnp.where(upper, qq[c] * a, 0.0).astype(BF16)
            ka = jnp.where(upper, 0.0, kk[c] * a).astype(BF16)
            s_l = _dot_nt(qa, ka)
            if l < levels:
                s_l = jnp.where(same_block, s_l, 0.0)
            scores[c] = s_l if scores[c] is None else scores[c] + s_l

    pending = []
    for c in chunks:
        b = e[c][0:chunk]
        b_rest = e[c][chunk:2 * chunk]
        vb = vv[c].astype(BF16)
        o = jnp.sum(qq[c] * kk[c], axis=-1, keepdims=True) * vv[c] + _dot(scores[c].astype(BF16), vb)
        q_dec = (qq[c] * jnp.exp(b)).astype(BF16)
        st_add = _dot_tn(vb, (kk[c] * jnp.exp(b_rest)).astype(BF16))
        st_dec = jnp.exp(b[chunk - 1:chunk, :])
        pending.append((o, q_dec, st_add, st_dec))

    st = st_ref[...]
    for c in range(n_chunks):
        rows = slice(c * chunk, (c + 1) * chunk)
        o, q_dec, st_add, st_dec = pending[c]
        o = o + _dot_nt(q_dec, st.astype(BF16))
        st = st * st_dec + st_add
        o = o * lax.rsqrt(jnp.mean(o * o, axis=-1, keepdims=True) + RMS_EPS) * nw
        o_ref[0, rows, :] = (o * _silu(g_ref[0, rows, :])).astype(o_ref.dtype)
    st_ref[...] = st


def hgrn2_core(proj, lb, norm_w):
    bsz, s, d4 = proj.shape
    d = d4 // 4
    nh = d // HG_DK
    tt = _pick(s, (512, 256, 128, 64))
    chunk = min(GLA_CHUNK, tt)
    tables, levels = _gla_tables(chunk)
    col = lambda part: (lambda b, h, t: (b, t, part * nh + h))
    blk = (1, tt, HG_DK)
    return pl.pallas_call(
        functools.partial(_gla_kernel, chunk=chunk, levels=levels, n_chunks=tt // chunk),
        grid=(bsz, nh, s // tt),
        in_specs=[
            pl.BlockSpec(blk, col(0)),
            pl.BlockSpec(blk, col(1)),
            pl.BlockSpec(blk, col(2)),
            pl.BlockSpec(blk, col(3)),
            pl.BlockSpec((1, HG_DK), lambda b, h, t: (0, h)),
            pl.BlockSpec((1, HG_DK), lambda b, h, t: (0, 0)),
            pl.BlockSpec(tables.shape, lambda b, h, t: (0, 0)),
        ],
        out_specs=pl.BlockSpec(blk, lambda b, h, t: (b, t, h)),
        out_shape=jax.ShapeDtypeStruct((bsz, s, d), BF16),
        scratch_shapes=[pltpu.VMEM((HG_DK, HG_DK), F32)],
        compiler_params=_cparams(("parallel", "parallel", "arbitrary")),
        name="hgrn2_core",
    )(proj, proj, proj, proj, lb.reshape(1, d), norm_w.reshape(1, HG_DK),
      jnp.asarray(tables, BF16))


def _lower_bounds_kernel(l_ref, o_ref):
    x = l_ref[...]
    e = jnp.exp(x - jnp.max(x, axis=0, keepdims=True))
    p = e / jnp.sum(e, axis=0, keepdims=True)
    n = x.shape[0]
    r = lax.broadcasted_iota(jnp.int32, (n, n), 0)
    c = lax.broadcasted_iota(jnp.int32, (n, n), 1)
    acc = jnp.zeros_like(p)
    for j in range(n):
        acc = acc + jnp.where(r[:, j:j + 1] >= j, 1.0, 0.0) * p[j:j + 1, :]
    del c
    o_ref[...] = acc


def lower_bounds(logits):
    return pl.pallas_call(
        _lower_bounds_kernel,
        out_shape=jax.ShapeDtypeStruct(logits.shape, F32),
        name="hgrn2_lower_bounds",
    )(logits)


def hgrn2_layer(x, mod, w_in, lb, norm_w, w_out, ln_w, ln_b, alpha):
    proj = mm_mod(x, mod, w_in.astype(BF16), shift_row=0)
    o = hgrn2_core(proj, lb, norm_w)
    return mm_res_ln(o, w_out.astype(BF16), x, mod, 2, ln_w, ln_b, alpha)


def _split_hi_lo(v):
    hi = v.astype(BF16)
    lo = (v - hi.astype(F32)).astype(BF16)
    return hi, lo


def _ssd_kernel(xbc_ref, z_ref, dt_ref, cw_ref, cb_ref, dtb_ref, alog_ref, dsk_ref, nw_ref,
                ltri_ref, exp_ref, o_ref, st_ref, ext_ref, xc_ref, *, tc, di, groups):
    gw = di // groups
    gn = groups * SSD_DSTATE
    ch = di + 2 * gn
    halo = 8

    @pl.when(pl.program_id(1) == 0)
    def _():
        st_ref[...] = jnp.zeros_like(st_ref)
        ext_ref[0:halo, :] = jnp.zeros((halo, ch), F32)

    ext_ref[halo:halo + tc, :] = xbc_ref[0]
    cblk = _pick(ch, (512, 256, 128))
    for j in range(ch // cblk):
        cs = slice(j * cblk, (j + 1) * cblk)
        acc = cb_ref[:, cs] + cw_ref[0:1, cs] * ext_ref[halo - 3:halo - 3 + tc, cs]
        for t in range(1, SSD_CONV):
            acc = acc + cw_ref[t:t + 1, cs] * ext_ref[halo - 3 + t:halo - 3 + t + tc, cs]
        xc_ref[:, cs] = _silu(acc)
    ext_ref[0:halo, :] = ext_ref[tc:tc + halo, :]

    dt = _softplus(dt_ref[0] + dtb_ref[...])
    da = dt * (-jnp.exp(alog_ref[...]))
    da_hi, da_lo = _split_hi_lo(da)
    a2 = _dot(ltri_ref[...], jnp.concatenate([da_hi, da_lo], axis=1))
    a = a2[:, :LANE] + a2[:, LANE:]
    a_t = a.T
    a_end = a[tc - 1:tc, :]
    a_hi, a_lo = _split_hi_lo(a)
    d_hi, d_lo = _split_hi_lo(dt)
    e_hi, e_lo = _split_hi_lo(a_end)
    stack = jnp.concatenate([a_hi, a_lo, d_hi, d_lo,
                             jnp.broadcast_to(e_hi, (8, LANE)), jnp.broadcast_to(e_lo, (8, LANE))], axis=0)
    row = lax.broadcasted_iota(jnp.int32, (tc, tc), 0)
    col = lax.broadcasted_iota(jnp.int32, (tc, tc), 1)
    causal = row >= col
    lane = lax.broadcasted_iota(jnp.int32, (tc, LANE), 1)
    low_half = lane < SSD_HEADDIM

    for g in range(groups):
        gs = slice(g * gw, (g + 1) * gw)
        ex = _dot(stack, exp_ref[:, gs])
        a_x = ex[0:tc] + ex[tc:2 * tc]
        dt_x = ex[2 * tc:3 * tc] + ex[3 * tc:4 * tc]
        ae_x = ex[4 * tc:4 * tc + 1] + ex[4 * tc + 8:4 * tc + 9]
        bg = xc_ref[:, di + g * SSD_DSTATE:di + (g + 1) * SSD_DSTATE].astype(BF16)
        cg = xc_ref[:, di + gn + g * SSD_DSTATE:di + gn + (g + 1) * SSD_DSTATE].astype(BF16)
        xs = xc_ref[:, gs]
        xdt = xs * dt_x
        cbm = _dot_nt(cg, bg)
        st = st_ref[g * SSD_DSTATE:(g + 1) * SSD_DSTATE, :]
        y = jnp.exp(a_x) * _dot(cg, st.astype(BF16)) + dsk_ref[:, gs] * xs
        tiles = []
        for j in range(gw // LANE):
            xt = xdt[:, j * LANE:(j + 1) * LANE]
            acc = None
            for half in range(2):
                h = (g * gw + j * LANE) // SSD_HEADDIM + half
                seg = a[:, h:h + 1] - a_t[h:h + 1, :]
                m = (cbm * jnp.exp(jnp.where(causal, seg, -1e30))).astype(BF16)
                xm = jnp.where(low_half if half == 0 else jnp.logical_not(low_half), xt, 0.0)
                part = _dot(m, xm.astype(BF16))
                acc = part if acc is None else acc + part
            tiles.append(acc)
        y = y + jnp.concatenate(tiles, axis=1)
        xw = (xdt * jnp.exp(ae_x - a_x)).astype(BF16)
        st_ref[g * SSD_DSTATE:(g + 1) * SSD_DSTATE, :] = st * jnp.exp(ae_x) + _dot_tn(bg, xw)
        y = y * _silu(z_ref[0, :, gs])
        y = y * lax.rsqrt(jnp.mean(y * y, axis=-1, keepdims=True) + RMS_EPS) * nw_ref[:, gs]
        o_ref[0, :, gs] = y.astype(o_ref.dtype)


def ssd_core(xbc, z, dt, conv_w, conv_b, dt_bias, a_log, d_skip, norm_w):
    bsz, s, ch = xbc.shape
    di = z.shape[-1]
    nh = di // SSD_HEADDIM
    groups = (ch - di) // (2 * SSD_DSTATE)
    gw = di // groups
    tc = min(SSD_CHUNK, s)
    pad_h = lambda v: jnp.pad(v, (0, LANE - nh)).reshape(1, LANE)
    ltri = jnp.asarray(np.tril(np.ones((tc, tc), np.float32)), BF16)
    expand = np.zeros((LANE, di), np.float32)
    expand[np.arange(di) // SSD_HEADDIM, np.arange(di)] = 1.0
    full = lambda shape: pl.BlockSpec(shape, lambda b, c: (0,) * len(shape))
    return pl.pallas_call(
        functools.partial(_ssd_kernel, tc=tc, di=di, groups=groups),
        grid=(bsz, s // tc),
        in_specs=[
            pl.BlockSpec((1, tc, ch), lambda b, c: (b, c, 0)),
            pl.BlockSpec((1, tc, di), lambda b, c: (b, c, 0)),
            pl.BlockSpec((1, tc, LANE), lambda b, c: (b, c, 0)),
            full((SSD_CONV, ch)), full((1, ch)), full((1, LANE)), full((1, LANE)),
            full((1, di)), full((1, di)), full((tc, tc)), full((LANE, di)),
        ],
        out_specs=pl.BlockSpec((1, tc, di), lambda b, c: (b, c, 0)),
        out_shape=jax.ShapeDtypeStruct((bsz, s, di), BF16),
        scratch_shapes=[
            pltpu.VMEM((groups * SSD_DSTATE, gw), F32),
            pltpu.VMEM((tc + 8, ch), F32),
            pltpu.VMEM((tc, ch), F32),
        ],
        compiler_params=_cparams(("parallel", "arbitrary")),
        name="ssd_core",
    )(xbc, z, dt, conv_w, conv_b.reshape(1, ch), pad_h(dt_bias), pad_h(a_log),
      jnp.repeat(d_skip, SSD_HEADDIM).reshape(1, di), norm_w.reshape(1, di), ltri,
      jnp.asarray(expand, BF16))


def mamba2_layer(x, mod, w_in, conv_w, conv_b, dt_bias, a_log, d_skip, norm_w, w_out,
                 ln_w, ln_b, alpha):
    di = w_out.shape[0]
    ch = conv_w.shape[1]
    wb = w_in.astype(BF16)
    z = mm_mod(x, mod, wb[:, :di], shift_row=0)
    xbc = mm_mod(x, mod, wb[:, di:di + ch], shift_row=0)
    dt = mm_mod(x, mod, _pad_cols(wb[:, di + ch:], LANE), shift_row=0)
    y = ssd_core(xbc, z, dt, conv_w, conv_b, dt_bias, a_log, d_skip, norm_w)
    return mm_res_ln(y, w_out.astype(BF16), x, mod, 2, ln_w, ln_b, alpha)


DA_BATCH = 4


def _rows(start, size, stride):
    return pl.ds(start, size) if stride == 1 else pl.ds(start, size, stride=stride)


def _da_kernel(*refs, tile, n_heads, scale):
    n_grp = len(DA_PATTERNS)
    ins = refs[:5 * n_grp]
    o_ref, m_ref, l_ref, acc_ref = refs[5 * n_grp:]
    span = DA_SPAN
    first_tile = pl.program_id(2) == 0
    head = jnp.full((1, 2 * span), pl.program_id(1), jnp.int32).astype(F32)
    slope = jnp.exp((head + 1.0) * (-8.0 * math.log(2.0) / n_heads))
    qi = lax.broadcasted_iota(jnp.int32, (span, 2 * span), 0)
    ki = lax.broadcasted_iota(jnp.int32, (span, 2 * span), 1)
    dist = qi + span - ki
    valid = (dist >= 0) & (dist <= span)
    valid_first = valid & jnp.logical_or(ki >= span, jnp.logical_not(first_tile))
    neg = -1e30

    for g, (window, dil) in enumerate(DA_PATTERNS):
        q_ref, kc_ref, vc_ref, kp_ref, vp_ref = ins[5 * g:5 * g + 5]
        bias = dist.astype(F32) * (slope * (-float(dil)))
        blocks = [(r, i) for r in range(dil) for i in range(tile // (span * dil))]
        for b0 in range(0, len(blocks), DA_BATCH):
            batch = blocks[b0:b0 + DA_BATCH]
            qrows, vs, ss = [], [], []
            for r, i in batch:
                rows = _rows(i * span * dil + r, span, dil)
                q = (q_ref[0, rows, :] * scale).astype(BF16)
                if i == 0:
                    prow = _rows(r, span, dil)
                    k = jnp.concatenate([kp_ref[0, prow, :], kc_ref[0, prow, :]], axis=0)
                    v = jnp.concatenate([vp_ref[0, prow, :], vc_ref[0, prow, :]], axis=0)
                    ok = valid_first
                else:
                    krows = _rows((i - 1) * span * dil + r, 2 * span, dil)
                    k = kc_ref[0, krows, :]
                    v = vc_ref[0, krows, :]
                    ok = valid
                qrows.append(rows)
                vs.append(v.astype(BF16))
                ss.append(jnp.where(ok, _dot_nt(q, k.astype(BF16)) + bias, neg))
            stats = []
            for rows, s in zip(qrows, ss):
                m_blk = jnp.max(s, axis=-1, keepdims=True)
                if g == 0:
                    m_new, corr = m_blk, None
                else:
                    m_old = m_ref[rows, :]
                    m_new = jnp.maximum(m_old, m_blk)
                    corr = jnp.exp(m_old - m_new)
                p = jnp.exp(s - m_new)
                stats.append((m_new, corr, jnp.sum(p, axis=-1, keepdims=True), p.astype(BF16)))
            pvs = [_dot(st[3], v) for st, v in zip(stats, vs)]
            for rows, (m_new, corr, l_blk, _), pv in zip(qrows, stats, pvs):
                if g == 0:
                    l_new, acc_new = l_blk, pv
                else:
                    l_new = corr * l_ref[rows, :] + l_blk
                    acc_new = corr * acc_ref[rows, :] + pv
                m_ref[rows, :] = m_new
                l_ref[rows, :] = l_new
                acc_ref[rows, :] = acc_new

    o_ref[0] = (acc_ref[...] / l_ref[...]).astype(o_ref.dtype)


def dilated_attention_core(qkv, n_heads):
    bsz, s, _ = qkv.shape
    hd = DA_HEADDIM
    tile = max(w for w, _ in DA_PATTERNS)
    assert s % tile == 0 and all(w // d == DA_SPAN for w, d in DA_PATTERNS)
    in_specs, args = [], []
    for g, (window, dil) in enumerate(DA_PATTERNS):
        prev_rows = DA_SPAN * dil
        per_tile = tile // prev_rows
        for part in (0, 1, 2):
            colb = (g * 3 + part) * n_heads
            in_specs.append(pl.BlockSpec((1, tile, hd), lambda b, h, t, colb=colb: (b, t, colb + h)))
            args.append(qkv)
            if part == 0:
                continue
        for part in (1, 2):
            colb = (g * 3 + part) * n_heads
            in_specs.append(pl.BlockSpec(
                (1, prev_rows, hd),
                lambda b, h, t, colb=colb, per_tile=per_tile: (b, jnp.maximum(t * per_tile - 1, 0), colb + h)))
            args.append(qkv)
    return pl.pallas_call(
        functools.partial(_da_kernel, tile=tile, n_heads=n_heads, scale=hd ** -0.5),
        grid=(bsz, n_heads, s // tile),
        in_specs=in_specs,
        out_specs=pl.BlockSpec((1, tile, hd), lambda b, h, t: (b, t, h)),
        out_shape=jax.ShapeDtypeStruct((bsz, s, n_heads * hd), BF16),
        scratch_shapes=[
            pltpu.VMEM((tile, 1), F32),
            pltpu.VMEM((tile, 1), F32),
            pltpu.VMEM((tile, hd), F32),
        ],
        compiler_params=_cparams(("parallel", "parallel", "arbitrary")),
        name="dilated_attention",
    )(*args)


def dilated_attention_layer(x, mod, w_in, w_out, ln_w, ln_b, alpha):
    d = x.shape[-1]
    qkv = mm_mod(x, mod, w_in.astype(BF16), shift_row=0)
    o = dilated_attention_core(qkv, d // DA_HEADDIM)
    return mm_res_ln(o, w_out.astype(BF16), x, mod, 2, ln_w, ln_b, alpha)


RW_TN = 256


RW_N_MIX = 6


def _rw_mix_kernel(x_ref, xp_ref, mod_ref, mu_ref, w_ref, o_ref, mix_scr, *, tiles_per_proj):
    j = pl.program_id(2)

    @pl.when(j == 0)
    def _():
        shift = mod_ref[0, 0:1, :]
        scale = 1.0 + mod_ref[0, 1:2, :]
        h = x_ref[0] * scale + shift
        prev_row = xp_ref[0, 7:8, :] * scale + shift
        prev_row = jnp.where(pl.program_id(1) == 0, 0.0, prev_row)
        rolled = pltpu.roll(h, 1, 0)
        row = lax.broadcasted_iota(jnp.int32, h.shape, 0)
        xx = jnp.where(row == 0, prev_row, rolled) - h
        for m in range(RW_N_MIX):
            mix_scr[m] = (h + xx * mu_ref[m:m + 1, :]).astype(BF16)

    n_wide = 3 * tiles_per_proj
    mix = jnp.where(j < n_wide, j // tiles_per_proj, 3 + j - n_wide)
    o_ref[0] = _dot(mix_scr[mix], w_ref[...])


def rw_mix_mm(x, mod, mu, w_all):
    bsz, s, d = x.shape
    n = w_all.shape[1]
    tm = _pick(s, (512, 256, 128))
    return pl.pallas_call(
        functools.partial(_rw_mix_kernel, tiles_per_proj=d // RW_TN),
        grid=(bsz, s // tm, n // RW_TN),
        in_specs=[
            pl.BlockSpec((1, tm, d), lambda b, i, j: (b, i, 0)),
            pl.BlockSpec((1, 8, d), lambda b, i, j: (b, jnp.maximum(i * (tm // 8) - 1, 0), 0)),
            pl.BlockSpec((1, N_MOD, d), lambda b, i, j: (b, 0, 0)),
            pl.BlockSpec((RW_N_MIX, d), lambda b, i, j: (0, 0)),
            pl.BlockSpec((d, RW_TN), lambda b, i, j: (0, j)),
        ],
        out_specs=pl.BlockSpec((1, tm, RW_TN), lambda b, i, j: (b, i, j)),
        out_shape=jax.ShapeDtypeStruct((bsz, s, n), F32),
        scratch_shapes=[pltpu.VMEM((RW_N_MIX, tm, d), BF16)],
        compiler_params=_cparams(("parallel", "parallel", "arbitrary")),
        name="rw_mix_mm",
    )(x, x, mod, mu, w_all)


def _rw_lora2_kernel(w1h_ref, a1h_ref, g1h_ref, w2_ref, a2_ref, g2_ref, w0_ref, a0_ref,
                     wp_ref, ap_ref, gt_ref):
    wp_ref[0] = w0_ref[...] + _dot(jnp.tanh(w1h_ref[0]).astype(BF16), w2_ref[...])
    ap_ref[0] = a0_ref[...] + _dot(a1h_ref[0].astype(BF16), a2_ref[...])
    gt_ref[0] = _dot(_sigmoid(g1h_ref[0]).astype(BF16), g2_ref[...])


def rw_lora2(proj, col0, w2, a2, g2, w0, a0):
    bsz, s, _ = proj.shape
    d = w2.shape[1]
    tm = _pick(s, (256, 128))
    cb = col0 // RW_TN
    hspec = lambda k: pl.BlockSpec((1, tm, RW_TN), lambda b, i, k=k: (b, i, cb + k))
    wspec = pl.BlockSpec((RW_TN, d), lambda b, i: (0, 0))
    vspec = pl.BlockSpec((1, d), lambda b, i: (0, 0))
    ospec = pl.BlockSpec((1, tm, d), lambda b, i: (b, i, 0))
    oshape = jax.ShapeDtypeStruct((bsz, s, d), F32)
    return pl.pallas_call(
        _rw_lora2_kernel,
        grid=(bsz, s // tm),
        in_specs=[hspec(0), hspec(1), hspec(2), wspec, wspec, wspec, vspec, vspec],
        out_specs=[ospec, ospec, ospec],
        out_shape=[oshape, oshape, oshape],
        compiler_params=_cparams(("parallel", "parallel")),
        name="rw_lora2",
    )(proj, proj, proj, w2, a2, g2, w0.reshape(1, d), a0.reshape(1, d))


def _rw_scan_kernel(r_ref, k_ref, v_ref, wp_ref, ap_ref, kk_ref, ka_ref, rk_ref, lnw_ref, lnb_ref,
                    o_ref, st_ref, dec_ref, kkn_ref, bb_ref, kh_ref, *, tb, lw):
    hd = RW_HEADDIM
    bh = r_ref.shape[-1]

    @pl.when(pl.program_id(0) == 0)
    def _():
        st_ref[...] = jnp.zeros_like(st_ref)

    def prep(t, carry):
        a = _sigmoid(ap_ref[t])
        kraw = k_ref[t]
        kk = kraw * kk_ref[...]
        nrm = jnp.sum(kk * kk, axis=0, keepdims=True)
        kkn = kk * lax.rsqrt(jnp.maximum(nrm, 1e-24))
        dec_ref[t] = jnp.exp(-jnp.exp(-_softplus(-wp_ref[t]) - 0.5))
        kkn_ref[t] = kkn
        bb_ref[t] = kkn * a
        kh_ref[t] = kraw * (1.0 + (a - 1.0) * ka_ref[...])
        return carry

    lax.fori_loop(0, tb, prep, 0)

    sub = 8

    def rows8(ref, t, kb, lanes):
        return ref[t, pl.ds(pl.multiple_of(kb * sub, sub), sub), lanes]

    def bcast(tile, j):
        return jnp.broadcast_to(tile[j:j + 1, :], (hd, lw))

    def step(t, carry):
        for c in range(bh // lw):
            lanes = slice(c * lw, (c + 1) * lw)

            def pass1(kb, sa):
                a8 = rows8(kkn_ref, t, kb, lanes)
                for j in range(sub):
                    sa = sa + st_ref[kb * sub + j, :, lanes] * bcast(a8, j)
                return sa

            sa = -lax.fori_loop(0, hd // sub, pass1, jnp.zeros((hd, lw), F32))
            vt = v_ref[t, :, lanes]

            def pass2(kb, y):
                d8 = rows8(dec_ref, t, kb, lanes)
                b8 = rows8(bb_ref, t, kb, lanes)
                k8 = rows8(kh_ref, t, kb, lanes)
                r8 = rows8(r_ref, t, kb, lanes)
                for j in range(sub):
                    s_new = (st_ref[kb * sub + j, :, lanes] * bcast(d8, j)
                             + sa * bcast(b8, j) + vt * bcast(k8, j))
                    st_ref[kb * sub + j, :, lanes] = s_new
                    y = y + s_new * bcast(r8, j)
                return y

            y = lax.fori_loop(0, hd // sub, pass2, jnp.zeros((hd, lw), F32))
            mu = jnp.mean(y, axis=0, keepdims=True)
            yc = y - mu
            var = jnp.mean(yc * yc, axis=0, keepdims=True)
            yn = yc * lax.rsqrt(var + RW_GN_EPS) * lnw_ref[:, lanes] + lnb_ref[:, lanes]
            bonus = jnp.sum(r_ref[t, :, lanes] * kh_ref[t, :, lanes] * rk_ref[:, lanes],
                            axis=0, keepdims=True)
            o_ref[t, :, lanes] = yn + bonus * vt
        return carry

    lax.fori_loop(0, tb, step, 0)


def rw_scan(r, k, v, wpre, apre, k_k, k_a, r_k, ln_w, ln_b):
    s, hd, bh = r.shape
    tb = _pick(s, (16, 8))
    lw = min(LANE, bh)
    seq = pl.BlockSpec((tb, hd, bh), lambda i: (i, 0, 0))
    par = pl.BlockSpec((hd, bh), lambda i: (0, 0))
    return pl.pallas_call(
        functools.partial(_rw_scan_kernel, tb=tb, lw=lw),
        grid=(s // tb,),
        in_specs=[seq] * 5 + [par] * 5,
        out_specs=seq,
        out_shape=jax.ShapeDtypeStruct((s, hd, bh), F32),
        scratch_shapes=[pltpu.VMEM((hd, hd, bh), F32)] + [pltpu.VMEM((tb, hd, bh), F32)] * 4,
        compiler_params=_cparams(("arbitrary",)),
        name="rw_scan",
    )(r, k, v, wpre, apre, k_k, k_a, r_k, ln_w, ln_b)


def rwkv7_layer(x, mod, mu, w_rkv, w0, w1, w2, a0, a1, a2, g1, g2, k_k, k_a, r_k, gn_w, gn_b,
                w_out, ln_w, ln_b, alpha):
    bsz, s, d = x.shape
    nh = d // RW_HEADDIM
    pad_c = lambda w: _pad_cols(w, RW_TN)
    pad_r = lambda w: jnp.pad(w, ((0, RW_TN - w.shape[0]), (0, 0)))
    secs = [w_rkv[0], w_rkv[1], w_rkv[2], pad_c(w1), pad_c(a1), pad_c(g1)]
    w_all = jnp.concatenate(secs, axis=1).astype(BF16)
    proj = rw_mix_mm(x, mod, mu, w_all)
    wpre, apre, gate = rw_lora2(proj, 3 * d, pad_r(w2).astype(BF16), pad_r(a2).astype(BF16),
                                pad_r(g2).astype(BF16), w0, a0)

    def to_scan(t):
        return t.reshape(bsz, s, nh, RW_HEADDIM).transpose(1, 3, 0, 2).reshape(s, RW_HEADDIM, bsz * nh)

    def par(p):
        return jnp.tile(p.reshape(nh, RW_HEADDIM).T[:, None, :], (1, bsz, 1)).reshape(RW_HEADDIM, bsz * nh)

    y = rw_scan(to_scan(proj[..., 0:d]), to_scan(proj[..., d:2 * d]), to_scan(proj[..., 2 * d:3 * d]),
                to_scan(wpre), to_scan(apre), par(k_k), par(k_a), par(r_k.reshape(-1)),
                par(gn_w), par(gn_b))
    y = y.reshape(s, RW_HEADDIM, bsz, nh).transpose(2, 0, 3, 1).reshape(bsz, s, d)
    return mm_res_ln(y, w_out.astype(BF16), x, mod, 2, ln_w, ln_b, alpha, h2=gate)


def kernel(x, c, ada_w, ada_b, ln_w, ln_b, hg_w_in, hg_lb_logits, hg_norm_w, hg_w_out, ssd_w_in, ssd_conv_w, ssd_conv_b, ssd_dt_bias, ssd_a_log, ssd_d, ssd_norm_w, ssd_w_out, da_w_in, da_w_out, rw_mu, rw_w_rkv, rw_w0, rw_w1, rw_w2, rw_a0, rw_a1, rw_a2, rw_g1, rw_g2, rw_k_k, rw_k_a, rw_r_k, rw_ln_w, rw_ln_b, rw_w_out, ffn_w_gu, ffn_w_down, moe_router, moe_w_gu, moe_w_down):
    depth = ada_w.shape[0]
    bsz, _, d = x.shape
    alpha = (2.0 * depth) ** 0.25
    lbs = lower_bounds(hg_lb_logits)
    mods = ada_modulation(c, ada_w, ada_b).reshape(depth, bsz, N_MOD, d)
    for i in range(depth):
        mixer, j = i % 4, i // 4
        mod = mods[i]
        if mixer == 0:
            x = hgrn2_layer(x, mod, hg_w_in[j], lbs[i], hg_norm_w[j], hg_w_out[j],
                            ln_w[i, 0], ln_b[i, 0], alpha)
        elif mixer == 1:
            x = mamba2_layer(x, mod, ssd_w_in[j], ssd_conv_w[j], ssd_conv_b[j], ssd_dt_bias[j],
                             ssd_a_log[j], ssd_d[j], ssd_norm_w[j], ssd_w_out[j],
                             ln_w[i, 0], ln_b[i, 0], alpha)
        elif mixer == 2:
            x = dilated_attention_layer(x, mod, da_w_in[j], da_w_out[j], ln_w[i, 0], ln_b[i, 0], alpha)
        else:
            x = rwkv7_layer(x, mod, rw_mu[j], rw_w_rkv[j], rw_w0[j], rw_w1[j], rw_w2[j], rw_a0[j],
                            rw_a1[j], rw_a2[j], rw_g1[j], rw_g2[j], rw_k_k[j], rw_k_a[j], rw_r_k[j],
                            rw_ln_w[j], rw_ln_b[j], rw_w_out[j], ln_w[i, 0], ln_b[i, 0], alpha)
        if i % 2 == 0:
            x = dense_ffn(x, mod, ffn_w_gu[i // 2], ffn_w_down[i // 2], ln_w[i, 1], ln_b[i, 1], alpha)
        else:
            x = moe_ffn(x, mod, moe_router[i // 2], moe_w_gu[i // 2], moe_w_down[i // 2],
                        ln_w[i, 1], ln_b[i, 1], alpha)
    return x
```

```python
import functools
import math

import numpy as np
import jax
import jax.numpy as jnp
from jax import lax
from jax.experimental import pallas as pl
from jax.experimental.pallas import tpu as pltpu

F32 = jnp.float32
BF16 = jnp.bfloat16

N_MOD = 6
LN_EPS = 1e-5
RMS_EPS = 1e-6

HG_DK = 128
GLA_CHUNK = 64
SSD_HEADDIM = 64
SSD_DSTATE = 128
SSD_CONV = 4
SSD_CHUNK = 256
DA_PATTERNS = ((128, 1), (512, 4), (2048, 16))
DA_HEADDIM = 128
DA_SPAN = 128
RW_HEADDIM = 64
RW_GN_EPS = 64e-5
MOE_TOPK = 2

LANE = 128
VMEM_LIMIT = 56 * 1024 * 1024


def _pick(n, cands):
    for c in cands:
        if n % c == 0:
            return c
    return n


def _cparams(sem):
    return pltpu.CompilerParams(dimension_semantics=sem, vmem_limit_bytes=VMEM_LIMIT)


def _sigmoid(x):
    return 1.0 / (1.0 + jnp.exp(-x))


def _silu(x):
    return x * _sigmoid(x)


def _softplus(x):
    return jnp.maximum(x, 0.0) + jnp.log(1.0 + jnp.exp(-jnp.abs(x)))


def _dot(a, b):
    return jnp.dot(a, b, preferred_element_type=F32)


def _dot_nt(a, b):
    return lax.dot_general(a, b, (((1,), (1,)), ((), ())), preferred_element_type=F32)


def _dot_tn(a, b):
    return lax.dot_general(a, b, (((0,), (0,)), ((), ())), preferred_element_type=F32)


def _ada_kernel(c_ref, w_ref, b_ref, o_ref):
    o_ref[0] = _dot(c_ref[...].astype(BF16), w_ref[0].astype(BF16)) + b_ref[0]


def ada_modulation(c, ada_w, ada_b):
    depth, d, n = ada_w.shape
    bsz = c.shape[0]
    tn = _pick(n, (1024, 512, 256, 128))
    return pl.pallas_call(
        _ada_kernel,
        grid=(depth, n // tn),
        in_specs=[
            pl.BlockSpec((bsz, d), lambda i, j: (0, 0)),
            pl.BlockSpec((1, d, tn), lambda i, j: (i, 0, j)),
            pl.BlockSpec((1, 1, tn), lambda i, j: (i, 0, j)),
        ],
        out_specs=pl.BlockSpec((1, bsz, tn), lambda i, j: (i, 0, j)),
        out_shape=jax.ShapeDtypeStruct((depth, bsz, n), F32),
        compiler_params=_cparams(("parallel", "parallel")),
        name="ada_modulation",
    )(c, ada_w, ada_b.reshape(depth, 1, n))


def _mm_mod_kernel(x_ref, mod_ref, w_ref, o_ref, h_scr, *, shift_row):
    @pl.when(pl.program_id(2) == 0)
    def _():
        shift = mod_ref[0, shift_row:shift_row + 1, :]
        scale = mod_ref[0, shift_row + 1:shift_row + 2, :]
        h_scr[...] = (x_ref[0] * (1.0 + scale) + shift).astype(BF16)

    o_ref[0] = _dot(h_scr[...], w_ref[...]).astype(o_ref.dtype)


def mm_mod(x, mod, w, shift_row, out_dtype=F32):
    bsz, s, d = x.shape
    n = w.shape[1]
    tm = _pick(s, (1024, 512, 256, 128))
    tn = _pick(n, (1024, 512, 256, 128))
    return pl.pallas_call(
        functools.partial(_mm_mod_kernel, shift_row=shift_row),
        grid=(bsz, s // tm, n // tn),
        in_specs=[
            pl.BlockSpec((1, tm, d), lambda b, i, j: (b, i, 0)),
            pl.BlockSpec((1, N_MOD, d), lambda b, i, j: (b, 0, 0)),
            pl.BlockSpec((d, tn), lambda b, i, j: (0, j)),
        ],
        out_specs=pl.BlockSpec((1, tm, tn), lambda b, i, j: (b, i, j)),
        out_shape=jax.ShapeDtypeStruct((bsz, s, n), out_dtype),
        scratch_shapes=[pltpu.VMEM((tm, d), BF16)],
        compiler_params=_cparams(("parallel", "parallel", "arbitrary")),
        name="mm_mod",
    )(x, mod, w)


def _mm_res_ln_kernel(*refs, gate_row, has_h2, alpha):
    if has_h2:
        h_ref, h2_ref, w_ref, x_ref, mod_ref, lnw_ref, lnb_ref, o_ref = refs
        h = (h_ref[0] * h2_ref[0]).astype(BF16)
    else:
        h_ref, w_ref, x_ref, mod_ref, lnw_ref, lnb_ref, o_ref = refs
        h = h_ref[0].astype(BF16)
    gate = mod_ref[0, gate_row:gate_row + 1, :]
    z = alpha * x_ref[0] + (1.0 + gate) * _dot(h, w_ref[...])
    mu = jnp.mean(z, axis=-1, keepdims=True)
    zc = z - mu
    var = jnp.mean(zc * zc, axis=-1, keepdims=True)
    o_ref[0] = zc * lax.rsqrt(var + LN_EPS) * lnw_ref[...] + lnb_ref[...]


def mm_res_ln(h, w, x, mod, gate_row, ln_w, ln_b, alpha, h2=None):
    bsz, s, d = x.shape
    kdim = h.shape[-1]
    tm = _pick(s, (512, 256, 128) if kdim <= 2 * d else (256, 128))
    h_spec = pl.BlockSpec((1, tm, kdim), lambda b, i: (b, i, 0))
    ins = [h] + ([h2] if h2 is not None else [])
    in_specs = [h_spec] * len(ins) + [
        pl.BlockSpec((kdim, d), lambda b, i: (0, 0), pipeline_mode=pl.Buffered(1)),
        pl.BlockSpec((1, tm, d), lambda b, i: (b, i, 0)),
        pl.BlockSpec((1, N_MOD, d), lambda b, i: (b, 0, 0)),
        pl.BlockSpec((1, d), lambda b, i: (0, 0)),
        pl.BlockSpec((1, d), lambda b, i: (0, 0)),
    ]
    return pl.pallas_call(
        functools.partial(_mm_res_ln_kernel, gate_row=gate_row, has_h2=h2 is not None, alpha=alpha),
        grid=(bsz, s // tm),
        in_specs=in_specs,
        out_specs=pl.BlockSpec((1, tm, d), lambda b, i: (b, i, 0)),
        out_shape=jax.ShapeDtypeStruct((bsz, s, d), F32),
        compiler_params=_cparams(("parallel", "parallel")),
        name="mm_res_ln",
    )(*ins, w, x, mod, ln_w.reshape(1, d), ln_b.reshape(1, d))


def _mm_swiglu_kernel(x_ref, mod_ref, wg_ref, wu_ref, o_ref, h_scr, *, shift_row):
    @pl.when(pl.program_id(2) == 0)
    def _():
        shift = mod_ref[0, shift_row:shift_row + 1, :]
        scale = mod_ref[0, shift_row + 1:shift_row + 2, :]
        h_scr[...] = (x_ref[0] * (1.0 + scale) + shift).astype(BF16)

    h = h_scr[...]
    g = _dot(h, wg_ref[...])
    u = _dot(h, wu_ref[...])
    o_ref[0] = (_silu(g) * u).astype(o_ref.dtype)


def mm_swiglu(x, mod, wg, wu, shift_row):
    bsz, s, d = x.shape
    f = wg.shape[1]
    tm = _pick(s, (1024, 512, 256, 128))
    tn = _pick(f, (512, 256, 128))
    return pl.pallas_call(
        functools.partial(_mm_swiglu_kernel, shift_row=shift_row),
        grid=(bsz, s // tm, f // tn),
        in_specs=[
            pl.BlockSpec((1, tm, d), lambda b, i, j: (b, i, 0)),
            pl.BlockSpec((1, N_MOD, d), lambda b, i, j: (b, 0, 0)),
            pl.BlockSpec((d, tn), lambda b, i, j: (0, j)),
            pl.BlockSpec((d, tn), lambda b, i, j: (0, j)),
        ],
        out_specs=pl.BlockSpec((1, tm, tn), lambda b, i, j: (b, i, j)),
        out_shape=jax.ShapeDtypeStruct((bsz, s, f), BF16),
        scratch_shapes=[pltpu.VMEM((tm, d), BF16)],
        compiler_params=_cparams(("parallel", "parallel", "arbitrary")),
        name="mm_swiglu",
    )(x, mod, wg, wu)


def _pad_cols(w, mult):
    n = w.shape[-1]
    pad = (-n) % mult
    return jnp.pad(w, ((0, 0), (0, pad))) if pad else w


def dense_ffn(x, mod, w_gu, w_down, ln_w, ln_b, alpha):
    f = w_gu.shape[1] // 2
    fmult = 512 if f >= 512 else LANE
    wg = _pad_cols(w_gu[:, :f], fmult).astype(BF16)
    wu = _pad_cols(w_gu[:, f:], fmult).astype(BF16)
    wd = jnp.pad(w_down, ((0, wg.shape[1] - f), (0, 0))).astype(BF16)
    hmid = mm_swiglu(x, mod, wg, wu, shift_row=3)
    return mm_res_ln(hmid, wd, x, mod, 5, ln_w, ln_b, alpha)


MOE_TM = 512
SEL_I1, SEL_I2, SEL_G1, SEL_G2, SEL_R1, SEL_R2 = range(6)


def _route_kernel(x_ref, mod_ref, wr_ref, ltri_ref, sel_ref, hm_ref, cnt_ref, cnt_scr, *, n_experts):
    @pl.when((pl.program_id(0) == 0) & (pl.program_id(1) == 0))
    def _():
        cnt_scr[...] = jnp.zeros_like(cnt_scr)

    shift = mod_ref[0, 3:4, :]
    scale = mod_ref[0, 4:5, :]
    h = x_ref[0] * (1.0 + scale) + shift
    hm_ref[0] = h
    logits = jnp.dot(h, wr_ref[...], preferred_element_type=F32, precision=lax.Precision.HIGHEST)
    lane = lax.broadcasted_iota(jnp.int32, logits.shape, 1)
    ninf = -jnp.inf
    lg = jnp.where(lane < n_experts, logits, ninf)
    m1 = jnp.max(lg, axis=-1, keepdims=True)
    i1 = jnp.min(jnp.where(lg == m1, lane, LANE), axis=-1, keepdims=True)
    lg2 = jnp.where(lane == i1, ninf, lg)
    m2 = jnp.max(lg2, axis=-1, keepdims=True)
    i2 = jnp.min(jnp.where(lg2 == m2, lane, LANE), axis=-1, keepdims=True)
    e2 = jnp.exp(m2 - m1)
    g1 = 1.0 / (1.0 + e2)
    hot1 = lane == i1
    hot2 = lane == i2
    chosen = jnp.where(hot1 | hot2, 1.0, 0.0)
    before = _dot(ltri_ref[...], chosen.astype(BF16)) + cnt_scr[...]
    r1 = jnp.sum(jnp.where(hot1, before, 0.0), axis=-1, keepdims=True)
    r2 = jnp.sum(jnp.where(hot2, before, 0.0), axis=-1, keepdims=True)
    cnt_scr[...] += jnp.sum(chosen, axis=0, keepdims=True)
    rec = jnp.zeros_like(logits)
    for slot, val in ((SEL_I1, i1.astype(F32)), (SEL_I2, i2.astype(F32)), (SEL_G1, g1),
                      (SEL_G2, e2 * g1), (SEL_R1, r1), (SEL_R2, r2)):
        rec = jnp.where(lane == slot, val, rec)
    sel_ref[0] = rec
    cnt_ref[...] = jnp.broadcast_to(cnt_scr[...], cnt_ref.shape)


def moe_route(x, mod, w_router):
    bsz, s, d = x.shape
    n_experts = w_router.shape[1]
    tm = _pick(s, (256, 128))
    ltri = jnp.asarray(np.tril(np.ones((tm, tm), np.float32), -1), BF16)
    return pl.pallas_call(
        functools.partial(_route_kernel, n_experts=n_experts),
        grid=(bsz, s // tm),
        in_specs=[
            pl.BlockSpec((1, tm, d), lambda b, i: (b, i, 0)),
            pl.BlockSpec((1, N_MOD, d), lambda b, i: (b, 0, 0)),
            pl.BlockSpec((d, LANE), lambda b, i: (0, 0)),
            pl.BlockSpec((tm, tm), lambda b, i: (0, 0)),
        ],
        out_specs=[
            pl.BlockSpec((1, tm, LANE), lambda b, i: (b, i, 0)),
            pl.BlockSpec((1, tm, d), lambda b, i: (b, i, 0)),
            pl.BlockSpec((8, LANE), lambda b, i: (0, 0)),
        ],
        out_shape=[
            jax.ShapeDtypeStruct((bsz, s, LANE), F32),
            jax.ShapeDtypeStruct((bsz, s, d), F32),
            jax.ShapeDtypeStruct((8, LANE), F32),
        ],
        scratch_shapes=[pltpu.VMEM((1, LANE), F32)],
        compiler_params=_cparams(("arbitrary", "arbitrary")),
        name="moe_route",
    )(x, mod, _pad_cols(w_router, LANE), ltri)


def _row_copy(src_ref, src_row, dst_ref, dst_row, sem):
    return pltpu.make_async_copy(src_ref.at[pl.ds(src_row, 1)], dst_ref.at[pl.ds(dst_row, 1)], sem)


def _dispatch_kernel(pos_ref, hm_ref, zero_ref, hs_ref, sem, *, tm):
    del zero_ref
    src = hm_ref.at[0]

    def start(r, c):
        for k in range(MOE_TOPK):
            _row_copy(src, r, hs_ref, pos_ref[0, 0, MOE_TOPK * r + k], sem).start()
        return c

    def wait(r, c):
        for k in range(MOE_TOPK):
            _row_copy(src, r, hs_ref, pos_ref[0, 0, MOE_TOPK * r + k], sem).wait()
        return c

    lax.fori_loop(0, tm, start, 0)
    lax.fori_loop(0, tm, wait, 0)


def moe_dispatch(hm, pos, n_rows):
    bsz, s, d = hm.shape
    tm = _pick(s, (256, 128))
    nt = s // tm
    pos_t = pos.reshape(bsz * nt, 1, MOE_TOPK * tm)
    return pl.pallas_call(
        functools.partial(_dispatch_kernel, tm=tm),
        grid=(bsz, nt),
        in_specs=[
            pl.BlockSpec((1, 1, MOE_TOPK * tm), lambda b, i: (b * nt + i, 0, 0), memory_space=pltpu.SMEM),
            pl.BlockSpec((1, tm, d), lambda b, i: (b, i, 0)),
            pl.BlockSpec(memory_space=pl.ANY),
        ],
        out_specs=pl.BlockSpec(memory_space=pl.ANY),
        out_shape=jax.ShapeDtypeStruct((n_rows, d), F32),
        scratch_shapes=[pltpu.SemaphoreType.DMA(())],
        input_output_aliases={2: 0},
        compiler_params=_cparams(("arbitrary", "arbitrary")),
        name="moe_dispatch",
    )(pos_t, hm, jnp.zeros((n_rows, d), F32))


def _moe_gu_kernel(te_ref, nu_ref, hs_ref, wg_ref, wu_ref, o_ref, h_scr):
    del te_ref

    @pl.when(pl.program_id(0) < nu_ref[0])
    def _():
        @pl.when(pl.program_id(1) == 0)
        def _():
            h_scr[...] = hs_ref[...].astype(BF16)

        h = h_scr[...]
        g = _dot(h, wg_ref[0])
        u = _dot(h, wu_ref[0])
        o_ref[...] = (_silu(g) * u).astype(o_ref.dtype)

    @pl.when(pl.program_id(0) >= nu_ref[0])
    def _():
        o_ref[...] = jnp.zeros_like(o_ref)


def _moe_down_kernel(te_ref, nu_ref, h_ref, w_ref, o_ref):
    del te_ref

    @pl.when(pl.program_id(1) < nu_ref[0])
    def _():
        o_ref[...] = _dot(h_ref[...], w_ref[0])

    @pl.when(pl.program_id(1) >= nu_ref[0])
    def _():
        o_ref[...] = jnp.zeros_like(o_ref)


def _moe_combine_kernel(pos_ref, ys_ref, sel_ref, x_ref, mod_ref, lnw_ref, lnb_ref, o_ref, buf, sem, *,
                        tm, alpha):
    def start(r, c):
        for k in range(MOE_TOPK):
            _row_copy(ys_ref, pos_ref[0, 0, MOE_TOPK * r + k], buf.at[k], r, sem).start()
        return c

    def wait(r, c):
        for k in range(MOE_TOPK):
            _row_copy(ys_ref, pos_ref[0, 0, MOE_TOPK * r + k], buf.at[k], r, sem).wait()
        return c

    lax.fori_loop(0, tm, start, 0)
    lax.fori_loop(0, tm, wait, 0)
    sel = sel_ref[0]
    lane = lax.broadcasted_iota(jnp.int32, sel.shape, 1)
    g1 = jnp.sum(jnp.where(lane == SEL_G1, sel, 0.0), axis=-1, keepdims=True)
    g2 = jnp.sum(jnp.where(lane == SEL_G2, sel, 0.0), axis=-1, keepdims=True)
    y = g1 * buf[0] + g2 * buf[1]
    z = alpha * x_ref[0] + (1.0 + mod_ref[0, 5:6, :]) * y
    mu = jnp.mean(z, axis=-1, keepdims=True)
    zc = z - mu
    var = jnp.mean(zc * zc, axis=-1, keepdims=True)
    o_ref[0] = zc * lax.rsqrt(var + LN_EPS) * lnw_ref[...] + lnb_ref[...]


def moe_ffn(x, mod, w_router, w_gu, w_down, ln_w, ln_b, alpha):
    bsz, s, d = x.shape
    n_experts, _, ff2 = w_gu.shape
    ff = ff2 // 2
    tmr = min(MOE_TM, s)
    n_rows = MOE_TOPK * bsz * s + n_experts * tmr
    n_tiles = n_rows // tmr

    sel, hm, cnt = moe_route(x, mod, w_router)

    counts = cnt[0, :n_experts].astype(jnp.int32)
    padded = (counts + tmr - 1) // tmr * tmr
    ends = jnp.cumsum(padded)
    offs = ends - padded
    tile_expert = jnp.minimum(
        jnp.searchsorted(ends, jnp.arange(n_tiles, dtype=jnp.int32) * tmr, side="right"),
        n_experts - 1).astype(jnp.int32)
    n_used = (ends[-1:] // tmr).astype(jnp.int32)
    ids = sel[..., SEL_I1:SEL_I2 + 1].astype(jnp.int32)
    ranks = sel[..., SEL_R1:SEL_R2 + 1].astype(jnp.int32)
    hot = ids[..., None] == jnp.arange(n_experts, dtype=jnp.int32)
    pos = jnp.sum(jnp.where(hot, offs, 0), axis=-1) + ranks

    hs = moe_dispatch(hm, pos, n_rows)

    tn = _pick(ff, (1024, 512, 256, 128))
    nt = ff // tn
    wb = w_gu.astype(BF16)
    live = lambda i, nu: jnp.minimum(i, nu[0] - 1)
    col = lambda i, j, nu: jnp.where(i < nu[0], j, nt - 1)
    hmid = pl.pallas_call(
        _moe_gu_kernel,
        grid_spec=pltpu.PrefetchScalarGridSpec(
            num_scalar_prefetch=2,
            grid=(n_tiles, nt),
            in_specs=[
                pl.BlockSpec((tmr, d), lambda i, j, te, nu: (live(i, nu), 0)),
                pl.BlockSpec((1, d, tn), lambda i, j, te, nu: (te[live(i, nu)], 0, col(i, j, nu))),
                pl.BlockSpec((1, d, tn), lambda i, j, te, nu: (te[live(i, nu)], 0, nt + col(i, j, nu))),
            ],
            out_specs=pl.BlockSpec((tmr, tn), lambda i, j, te, nu: (i, j)),
            scratch_shapes=[pltpu.VMEM((tmr, d), BF16)],
        ),
        out_shape=jax.ShapeDtypeStruct((n_rows, ff), BF16),
        compiler_params=_cparams(("arbitrary", "arbitrary")),
        name="moe_gate_up",
    )(tile_expert, n_used, hs, wb, wb)

    dn = _pick(d, (1024, 512, 256, 128))
    ys = pl.pallas_call(
        _moe_down_kernel,
        grid_spec=pltpu.PrefetchScalarGridSpec(
            num_scalar_prefetch=2,
            grid=(d // dn, n_tiles),
            in_specs=[
                pl.BlockSpec((tmr, ff), lambda n, i, te, nu: (live(i, nu), 0)),
                pl.BlockSpec((1, ff, dn), lambda n, i, te, nu: (te[live(i, nu)], 0, n)),
            ],
            out_specs=pl.BlockSpec((tmr, dn), lambda n, i, te, nu: (i, n)),
        ),
        out_shape=jax.ShapeDtypeStruct((n_rows, d), F32),
        compiler_params=_cparams(("arbitrary", "arbitrary")),
        name="moe_down",
    )(tile_expert, n_used, hmid, w_down.astype(BF16))

    tm = _pick(s, (256, 128))
    ntk = s // tm
    return pl.pallas_call(
        functools.partial(_moe_combine_kernel, tm=tm, alpha=alpha),
        grid=(bsz, ntk),
        in_specs=[
            pl.BlockSpec((1, 1, MOE_TOPK * tm), lambda b, i: (b * ntk + i, 0, 0), memory_space=pltpu.SMEM),
            pl.BlockSpec(memory_space=pl.ANY),
            pl.BlockSpec((1, tm, LANE), lambda b, i: (b, i, 0)),
            pl.BlockSpec((1, tm, d), lambda b, i: (b, i, 0)),
            pl.BlockSpec((1, N_MOD, d), lambda b, i: (b, 0, 0)),
            pl.BlockSpec((1, d), lambda b, i: (0, 0)),
            pl.BlockSpec((1, d), lambda b, i: (0, 0)),
        ],
        out_specs=pl.BlockSpec((1, tm, d), lambda b, i: (b, i, 0)),
        out_shape=jax.ShapeDtypeStruct((bsz, s, d), F32),
        scratch_shapes=[pltpu.VMEM((MOE_TOPK, tm, d), F32), pltpu.SemaphoreType.DMA(())],
        compiler_params=_cparams(("arbitrary", "arbitrary")),
        name="moe_combine",
    )(pos.reshape(bsz * ntk, 1, MOE_TOPK * tm), ys, sel, x, mod, ln_w.reshape(1, d), ln_b.reshape(1, d))


def _gla_tables(chunk):
    t = np.arange(chunk)[:, None]
    u = np.arange(chunk)[None, :]
    mats = [(u <= t), (u > t)]
    levels = int(math.log2(chunk))
    for l in range(1, levels + 1):
        half = 1 << (l - 1)
        anchor = ((t >> l) << l) + half - 1
        upper = (t & half) != 0
        mats.append(np.where(upper, (u > anchor) & (u <= t), (u > t) & (u <= anchor)))
    return np.concatenate(mats, axis=0).astype(np.float32), levels


def _gla_kernel(q_ref, f_ref, i_ref, g_ref, lb_ref, nw_ref, m_ref, o_ref, st_ref, *,
                chunk, levels, n_chunks):
    @pl.when(pl.program_id(2) == 0)
    def _():
        st_ref[...] = jnp.zeros_like(st_ref)

    lb = lb_ref[...]
    nw = nw_ref[...]
    row = lax.broadcasted_iota(jnp.int32, (chunk, HG_DK), 0)
    srow = lax.broadcasted_iota(jnp.int32, (chunk, chunk), 0)
    scol = lax.broadcasted_iota(jnp.int32, (chunk, chunk), 1)

    chunks = range(n_chunks)
    qq, kk, vv, e = [], [], [], []
    for c in chunks:
        rows = slice(c * chunk, (c + 1) * chunk)
        fg = lb + (1.0 - lb) * _sigmoid(f_ref[0, rows, :])
        logf = jnp.log(fg)
        kk.append(1.0 - fg)
        qq.append(_silu(q_ref[0, rows, :]))
        vv.append(i_ref[0, rows, :])
        g_hi = logf.astype(BF16)
        g_lo = (logf - g_hi.astype(F32)).astype(BF16)
        e2 = _dot(m_ref[...], jnp.concatenate([g_hi, g_lo], axis=1))
        e.append(e2[:, :HG_DK] + e2[:, HG_DK:])

    scores = [None] * n_chunks
    for l in range(1, levels + 1):
        half = 1 << (l - 1)
        upper = (row & half) != 0
        same_block = (srow >> l) == (scol >> l)
        for c in chunks:
            a = jnp.exp(e[c][(l + 1) * chunk:(l + 2) * chunk])
            qa = jnp.where(upper, qq[c] * a, 0.0).astype(BF16)
            ka = jnp.where(upper, 0.0, kk[c] * a).astype(BF16)
            s_l = _dot_nt(qa, ka)
            if l < levels:
                s_l = jnp.where(same_block, s_l, 0.0)
            scores[c] = s_l if scores[c] is None else scores[c] + s_l

    pending = []
    for c in chunks:
        b = e[c][0:chunk]
        b_rest = e[c][chunk:2 * chunk]
        vb = vv[c].astype(BF16)
        o = jnp.sum(qq[c] * kk[c], axis=-1, keepdims=True) * vv[c] + _dot(scores[c].astype(BF16), vb)
        q_dec = (qq[c] * jnp.exp(b)).astype(BF16)
        st_add = _dot_tn(vb, (kk[c] * jnp.exp(b_rest)).astype(BF16))
        st_dec = jnp.exp(b[chunk - 1:chunk, :])
        pending.append((o, q_dec, st_add, st_dec))

    st = st_ref[...]
    for c in range(n_chunks):
        rows = slice(c * chunk, (c + 1) * chunk)
        o, q_dec, st_add, st_dec = pending[c]
        o = o + _dot_nt(q_dec, st.astype(BF16))
        st = st * st_dec + st_add
        o = o * lax.rsqrt(jnp.mean(o * o, axis=-1, keepdims=True) + RMS_EPS) * nw
        o_ref[0, rows, :] = (o * _silu(g_ref[0, rows, :])).astype(o_ref.dtype)
    st_ref[...] = st


def hgrn2_core(proj, lb, norm_w):
    bsz, s, d4 = proj.shape
    d = d4 // 4
    nh = d // HG_DK
    tt = _pick(s, (512, 256, 128, 64))
    chunk = min(GLA_CHUNK, tt)
    tables, levels = _gla_tables(chunk)
    col = lambda part: (lambda b, h, t: (b, t, part * nh + h))
    blk = (1, tt, HG_DK)
    return pl.pallas_call(
        functools.partial(_gla_kernel, chunk=chunk, levels=levels, n_chunks=tt // chunk),
        grid=(bsz, nh, s // tt),
        in_specs=[
            pl.BlockSpec(blk, col(0)),
            pl.BlockSpec(blk, col(1)),
            pl.BlockSpec(blk, col(2)),
            pl.BlockSpec(blk, col(3)),
            pl.BlockSpec((1, HG_DK), lambda b, h, t: (0, h)),
            pl.BlockSpec((1, HG_DK), lambda b, h, t: (0, 0)),
            pl.BlockSpec(tables.shape, lambda b, h, t: (0, 0)),
        ],
        out_specs=pl.BlockSpec(blk, lambda b, h, t: (b, t, h)),
        out_shape=jax.ShapeDtypeStruct((bsz, s, d), BF16),
        scratch_shapes=[pltpu.VMEM((HG_DK, HG_DK), F32)],
        compiler_params=_cparams(("parallel", "parallel", "arbitrary")),
        name="hgrn2_core",
    )(proj, proj, proj, proj, lb.reshape(1, d), norm_w.reshape(1, HG_DK),
      jnp.asarray(tables, BF16))


def _lower_bounds_kernel(l_ref, o_ref):
    x = l_ref[...]
    e = jnp.exp(x - jnp.max(x, axis=0, keepdims=True))
    p = e / jnp.sum(e, axis=0, keepdims=True)
    n = x.shape[0]
    r = lax.broadcasted_iota(jnp.int32, (n, n), 0)
    c = lax.broadcasted_iota(jnp.int32, (n, n), 1)
    acc = jnp.zeros_like(p)
    for j in range(n):
        acc = acc + jnp.where(r[:, j:j + 1] >= j, 1.0, 0.0) * p[j:j + 1, :]
    del c
    o_ref[...] = acc


def lower_bounds(logits):
    return pl.pallas_call(
        _lower_bounds_kernel,
        out_shape=jax.ShapeDtypeStruct(logits.shape, F32),
        name="hgrn2_lower_bounds",
    )(logits)


def hgrn2_layer(x, mod, w_in, lb, norm_w, w_out, ln_w, ln_b, alpha):
    proj = mm_mod(x, mod, w_in.astype(BF16), shift_row=0)
    o = hgrn2_core(proj, lb, norm_w)
    return mm_res_ln(o, w_out.astype(BF16), x, mod, 2, ln_w, ln_b, alpha)


def _split_hi_lo(v):
    hi = v.astype(BF16)
    lo = (v - hi.astype(F32)).astype(BF16)
    return hi, lo


def _ssd_kernel(xbc_ref, z_ref, dt_ref, cw_ref, cb_ref, dtb_ref, alog_ref, dsk_ref, nw_ref,
                ltri_ref, exp_ref, o_ref, st_ref, ext_ref, xc_ref, *, tc, di, groups):
    gw = di // groups
    gn = groups * SSD_DSTATE
    ch = di + 2 * gn
    halo = 8

    @pl.when(pl.program_id(1) == 0)
    def _():
        st_ref[...] = jnp.zeros_like(st_ref)
        ext_ref[0:halo, :] = jnp.zeros((halo, ch), F32)

    ext_ref[halo:halo + tc, :] = xbc_ref[0]
    cblk = _pick(ch, (512, 256, 128))
    for j in range(ch // cblk):
        cs = slice(j * cblk, (j + 1) * cblk)
        acc = cb_ref[:, cs] + cw_ref[0:1, cs] * ext_ref[halo - 3:halo - 3 + tc, cs]
        for t in range(1, SSD_CONV):
            acc = acc + cw_ref[t:t + 1, cs] * ext_ref[halo - 3 + t:halo - 3 + t + tc, cs]
        xc_ref[:, cs] = _silu(acc)
    ext_ref[0:halo, :] = ext_ref[tc:tc + halo, :]

    dt = _softplus(dt_ref[0] + dtb_ref[...])
    da = dt * (-jnp.exp(alog_ref[...]))
    da_hi, da_lo = _split_hi_lo(da)
    a2 = _dot(ltri_ref[...], jnp.concatenate([da_hi, da_lo], axis=1))
    a = a2[:, :LANE] + a2[:, LANE:]
    a_t = a.T
    a_end = a[tc - 1:tc, :]
    a_hi, a_lo = _split_hi_lo(a)
    d_hi, d_lo = _split_hi_lo(dt)
    e_hi, e_lo = _split_hi_lo(a_end)
    stack = jnp.concatenate([a_hi, a_lo, d_hi, d_lo,
                             jnp.broadcast_to(e_hi, (8, LANE)), jnp.broadcast_to(e_lo, (8, LANE))], axis=0)
    row = lax.broadcasted_iota(jnp.int32, (tc, tc), 0)
    col = lax.broadcasted_iota(jnp.int32, (tc, tc), 1)
    causal = row >= col
    lane = lax.broadcasted_iota(jnp.int32, (tc, LANE), 1)
    low_half = lane < SSD_HEADDIM

    for g in range(groups):
        gs = slice(g * gw, (g + 1) * gw)
        ex = _dot(stack, exp_ref[:, gs])
        a_x = ex[0:tc] + ex[tc:2 * tc]
        dt_x = ex[2 * tc:3 * tc] + ex[3 * tc:4 * tc]
        ae_x = ex[4 * tc:4 * tc + 1] + ex[4 * tc + 8:4 * tc + 9]
        bg = xc_ref[:, di + g * SSD_DSTATE:di + (g + 1) * SSD_DSTATE].astype(BF16)
        cg = xc_ref[:, di + gn + g * SSD_DSTATE:di + gn + (g + 1) * SSD_DSTATE].astype(BF16)
        xs = xc_ref[:, gs]
        xdt = xs * dt_x
        cbm = _dot_nt(cg, bg)
        st = st_ref[g * SSD_DSTATE:(g + 1) * SSD_DSTATE, :]
        y = jnp.exp(a_x) * _dot(cg, st.astype(BF16)) + dsk_ref[:, gs] * xs
        tiles = []
        for j in range(gw // LANE):
            xt = xdt[:, j * LANE:(j + 1) * LANE]
            acc = None
            for half in range(2):
                h = (g * gw + j * LANE) // SSD_HEADDIM + half
                seg = a[:, h:h + 1] - a_t[h:h + 1, :]
                m = (cbm * jnp.exp(jnp.where(causal, seg, -1e30))).astype(BF16)
                xm = jnp.where(low_half if half == 0 else jnp.logical_not(low_half), xt, 0.0)
                part = _dot(m, xm.astype(BF16))
                acc = part if acc is None else acc + part
            tiles.append(acc)
        y = y + jnp.concatenate(tiles, axis=1)
        xw = (xdt * jnp.exp(ae_x - a_x)).astype(BF16)
        st_ref[g * SSD_DSTATE:(g + 1) * SSD_DSTATE, :] = st * jnp.exp(ae_x) + _dot_tn(bg, xw)
        y = y * _silu(z_ref[0, :, gs])
        y = y * lax.rsqrt(jnp.mean(y * y, axis=-1, keepdims=True) + RMS_EPS) * nw_ref[:, gs]
        o_ref[0, :, gs] = y.astype(o_ref.dtype)


def ssd_core(xbc, z, dt, conv_w, conv_b, dt_bias, a_log, d_skip, norm_w):
    bsz, s, ch = xbc.shape
    di = z.shape[-1]
    nh = di // SSD_HEADDIM
    groups = (ch - di) // (2 * SSD_DSTATE)
    gw = di // groups
    tc = min(SSD_CHUNK, s)
    pad_h = lambda v: jnp.pad(v, (0, LANE - nh)).reshape(1, LANE)
    ltri = jnp.asarray(np.tril(np.ones((tc, tc), np.float32)), BF16)
    expand = np.zeros((LANE, di), np.float32)
    expand[np.arange(di) // SSD_HEADDIM, np.arange(di)] = 1.0
    full = lambda shape: pl.BlockSpec(shape, lambda b, c: (0,) * len(shape))
    return pl.pallas_call(
        functools.partial(_ssd_kernel, tc=tc, di=di, groups=groups),
        grid=(bsz, s // tc),
        in_specs=[
            pl.BlockSpec((1, tc, ch), lambda b, c: (b, c, 0)),
            pl.BlockSpec((1, tc, di), lambda b, c: (b, c, 0)),
            pl.BlockSpec((1, tc, LANE), lambda b, c: (b, c, 0)),
            full((SSD_CONV, ch)), full((1, ch)), full((1, LANE)), full((1, LANE)),
            full((1, di)), full((1, di)), full((tc, tc)), full((LANE, di)),
        ],
        out_specs=pl.BlockSpec((1, tc, di), lambda b, c: (b, c, 0)),
        out_shape=jax.ShapeDtypeStruct((bsz, s, di), BF16),
        scratch_shapes=[
            pltpu.VMEM((groups * SSD_DSTATE, gw), F32),
            pltpu.VMEM((tc + 8, ch), F32),
            pltpu.VMEM((tc, ch), F32),
        ],
        compiler_params=_cparams(("parallel", "arbitrary")),
        name="ssd_core",
    )(xbc, z, dt, conv_w, conv_b.reshape(1, ch), pad_h(dt_bias), pad_h(a_log),
      jnp.repeat(d_skip, SSD_HEADDIM).reshape(1, di), norm_w.reshape(1, di), ltri,
      jnp.asarray(expand, BF16))


def mamba2_layer(x, mod, w_in, conv_w, conv_b, dt_bias, a_log, d_skip, norm_w, w_out,
                 ln_w, ln_b, alpha):
    di = w_out.shape[0]
    ch = conv_w.shape[1]
    wb = w_in.astype(BF16)
    z = mm_mod(x, mod, wb[:, :di], shift_row=0)
    xbc = mm_mod(x, mod, wb[:, di:di + ch], shift_row=0)
    dt = mm_mod(x, mod, _pad_cols(wb[:, di + ch:], LANE), shift_row=0)
    y = ssd_core(xbc, z, dt, conv_w, conv_b, dt_bias, a_log, d_skip, norm_w)
    return mm_res_ln(y, w_out.astype(BF16), x, mod, 2, ln_w, ln_b, alpha)


DA_BATCH = 4


def _rows(start, size, stride):
    return pl.ds(start, size) if stride == 1 else pl.ds(start, size, stride=stride)


def _da_kernel(*refs, tile, n_heads, scale):
    n_grp = len(DA_PATTERNS)
    ins = refs[:5 * n_grp]
    o_ref, m_ref, l_ref, acc_ref = refs[5 * n_grp:]
    span = DA_SPAN
    first_tile = pl.program_id(2) == 0
    head = jnp.full((1, 2 * span), pl.program_id(1), jnp.int32).astype(F32)
    slope = jnp.exp((head + 1.0) * (-8.0 * math.log(2.0) / n_heads))
    qi = lax.broadcasted_iota(jnp.int32, (span, 2 * span), 0)
    ki = lax.broadcasted_iota(jnp.int32, (span, 2 * span), 1)
    dist = qi + span - ki
    valid = (dist >= 0) & (dist <= span)
    valid_first = valid & jnp.logical_or(ki >= span, jnp.logical_not(first_tile))
    neg = -1e30

    for g, (window, dil) in enumerate(DA_PATTERNS):
        q_ref, kc_ref, vc_ref, kp_ref, vp_ref = ins[5 * g:5 * g + 5]
        bias = dist.astype(F32) * (slope * (-float(dil)))
        blocks = [(r, i) for r in range(dil) for i in range(tile // (span * dil))]
        for b0 in range(0, len(blocks), DA_BATCH):
            batch = blocks[b0:b0 + DA_BATCH]
            qrows, vs, ss = [], [], []
            for r, i in batch:
                rows = _rows(i * span * dil + r, span, dil)
                q = (q_ref[0, rows, :] * scale).astype(BF16)
                if i == 0:
                    prow = _rows(r, span, dil)
                    k = jnp.concatenate([kp_ref[0, prow, :], kc_ref[0, prow, :]], axis=0)
                    v = jnp.concatenate([vp_ref[0, prow, :], vc_ref[0, prow, :]], axis=0)
                    ok = valid_first
                else:
                    krows = _rows((i - 1) * span * dil + r, 2 * span, dil)
                    k = kc_ref[0, krows, :]
                    v = vc_ref[0, krows, :]
                    ok = valid
                qrows.append(rows)
                vs.append(v.astype(BF16))
                ss.append(jnp.where(ok, _dot_nt(q, k.astype(BF16)) + bias, neg))
            stats = []
            for rows, s in zip(qrows, ss):
                m_blk = jnp.max(s, axis=-1, keepdims=True)
                if g == 0:
                    m_new, corr = m_blk, None
                else:
                    m_old = m_ref[rows, :]
                    m_new = jnp.maximum(m_old, m_blk)
                    corr = jnp.exp(m_old - m_new)
                p = jnp.exp(s - m_new)
                stats.append((m_new, corr, jnp.sum(p, axis=-1, keepdims=True), p.astype(BF16)))
            pvs = [_dot(st[3], v) for st, v in zip(stats, vs)]
            for rows, (m_new, corr, l_blk, _), pv in zip(qrows, stats, pvs):
                if g == 0:
                    l_new, acc_new = l_blk, pv
                else:
                    l_new = corr * l_ref[rows, :] + l_blk
                    acc_new = corr * acc_ref[rows, :] + pv
                m_ref[rows, :] = m_new
                l_ref[rows, :] = l_new
                acc_ref[rows, :] = acc_new

    o_ref[0] = (acc_ref[...] / l_ref[...]).astype(o_ref.dtype)


def dilated_attention_core(qkv, n_heads):
    bsz, s, _ = qkv.shape
    hd = DA_HEADDIM
    tile = max(w for w, _ in DA_PATTERNS)
    assert s % tile == 0 and all(w // d == DA_SPAN for w, d in DA_PATTERNS)
    in_specs, args = [], []
    for g, (window, dil) in enumerate(DA_PATTERNS):
        prev_rows = DA_SPAN * dil
        per_tile = tile // prev_rows
        for part in (0, 1, 2):
            colb = (g * 3 + part) * n_heads
            in_specs.append(pl.BlockSpec((1, tile, hd), lambda b, h, t, colb=colb: (b, t, colb + h)))
            args.append(qkv)
            if part == 0:
                continue
        for part in (1, 2):
            colb = (g * 3 + part) * n_heads
            in_specs.append(pl.BlockSpec(
                (1, prev_rows, hd),
                lambda b, h, t, colb=colb, per_tile=per_tile: (b, jnp.maximum(t * per_tile - 1, 0), colb + h)))
            args.append(qkv)
    return pl.pallas_call(
        functools.partial(_da_kernel, tile=tile, n_heads=n_heads, scale=hd ** -0.5),
        grid=(bsz, n_heads, s // tile),
        in_specs=in_specs,
        out_specs=pl.BlockSpec((1, tile, hd), lambda b, h, t: (b, t, h)),
        out_shape=jax.ShapeDtypeStruct((bsz, s, n_heads * hd), BF16),
        scratch_shapes=[
            pltpu.VMEM((tile, 1), F32),
            pltpu.VMEM((tile, 1), F32),
            pltpu.VMEM((tile, hd), F32),
        ],
        compiler_params=_cparams(("parallel", "parallel", "arbitrary")),
        name="dilated_attention",
    )(*args)


def dilated_attention_layer(x, mod, w_in, w_out, ln_w, ln_b, alpha):
    d = x.shape[-1]
    qkv = mm_mod(x, mod, w_in.astype(BF16), shift_row=0)
    o = dilated_attention_core(qkv, d // DA_HEADDIM)
    return mm_res_ln(o, w_out.astype(BF16), x, mod, 2, ln_w, ln_b, alpha)


RW_TN = 256


RW_N_MIX = 6
RW_SLAB = 256


def _rw_mix_kernel(x_ref, xp_ref, mod_ref, mu_ref, w_ref, o_ref, mix_scr, *, tiles_per_proj):
    j = pl.program_id(2)

    @pl.when(j == 0)
    def _():
        shift = mod_ref[0, 0:1, :]
        scale = 1.0 + mod_ref[0, 1:2, :]
        tm = x_ref.shape[1]
        slab = min(RW_SLAB, tm)
        row = lax.broadcasted_iota(jnp.int32, (slab, x_ref.shape[2]), 0)
        for s0 in range(0, tm, slab):
            h = x_ref[0, s0:s0 + slab, :] * scale + shift
            if s0 == 0:
                prev_row = xp_ref[0, 7:8, :] * scale + shift
                prev_row = jnp.where(pl.program_id(1) == 0, 0.0, prev_row)
            else:
                prev_row = x_ref[0, s0 - 1:s0, :] * scale + shift
            xx = jnp.where(row == 0, prev_row, pltpu.roll(h, 1, 0)) - h
            for m in range(RW_N_MIX):
                mix_scr[m, s0:s0 + slab, :] = (h + xx * mu_ref[m:m + 1, :]).astype(BF16)

    n_wide = 3 * tiles_per_proj
    mix = jnp.where(j < n_wide, j // tiles_per_proj, 3 + j - n_wide)
    o_ref[0] = _dot(mix_scr[mix], w_ref[...])


def rw_mix_mm(x, mod, mu, w_all):
    bsz, s, d = x.shape
    n = w_all.shape[1]
    tm = _pick(s, (1024, 512, 256, 128))
    return pl.pallas_call(
        functools.partial(_rw_mix_kernel, tiles_per_proj=d // RW_TN),
        grid=(bsz, s // tm, n // RW_TN),
        in_specs=[
            pl.BlockSpec((1, tm, d), lambda b, i, j: (b, i, 0)),
            pl.BlockSpec((1, 8, d), lambda b, i, j: (b, jnp.maximum(i * (tm // 8) - 1, 0), 0)),
            pl.BlockSpec((1, N_MOD, d), lambda b, i, j: (b, 0, 0)),
            pl.BlockSpec((RW_N_MIX, d), lambda b, i, j: (0, 0)),
            pl.BlockSpec((d, RW_TN), lambda b, i, j: (0, j)),
        ],
        out_specs=pl.BlockSpec((1, tm, RW_TN), lambda b, i, j: (b, i, j)),
        out_shape=jax.ShapeDtypeStruct((bsz, s, n), F32),
        scratch_shapes=[pltpu.VMEM((RW_N_MIX, tm, d), BF16)],
        compiler_params=_cparams(("parallel", "parallel", "arbitrary")),
        name="rw_mix_mm",
    )(x, x, mod, mu, w_all)


def _rw_lora2_kernel(w1h_ref, a1h_ref, g1h_ref, w2_ref, a2_ref, g2_ref, w0_ref, a0_ref,
                     wp_ref, ap_ref, gt_ref):
    wp_ref[0] = w0_ref[...] + _dot(jnp.tanh(w1h_ref[0]).astype(BF16), w2_ref[...])
    ap_ref[0] = a0_ref[...] + _dot(a1h_ref[0].astype(BF16), a2_ref[...])
    gt_ref[0] = _dot(_sigmoid(g1h_ref[0]).astype(BF16), g2_ref[...])


def rw_lora2(proj, col0, w2, a2, g2, w0, a0):
    bsz, s, _ = proj.shape
    d = w2.shape[1]
    tm = _pick(s, (256, 128))
    cb = col0 // RW_TN
    hspec = lambda k: pl.BlockSpec((1, tm, RW_TN), lambda b, i, k=k: (b, i, cb + k))
    wspec = pl.BlockSpec((RW_TN, d), lambda b, i: (0, 0))
    vspec = pl.BlockSpec((1, d), lambda b, i: (0, 0))
    ospec = pl.BlockSpec((1, tm, d), lambda b, i: (b, i, 0))
    oshape = jax.ShapeDtypeStruct((bsz, s, d), F32)
    return pl.pallas_call(
        _rw_lora2_kernel,
        grid=(bsz, s // tm),
        in_specs=[hspec(0), hspec(1), hspec(2), wspec, wspec, wspec, vspec, vspec],
        out_specs=[ospec, ospec, ospec],
        out_shape=[oshape, oshape, oshape],
        compiler_params=_cparams(("parallel", "parallel")),
        name="rw_lora2",
    )(proj, proj, proj, w2, a2, g2, w0.reshape(1, d), a0.reshape(1, d))


def _rw_scan_kernel(r_ref, k_ref, v_ref, wp_ref, ap_ref, kk_ref, ka_ref, rk_ref, lnw_ref, lnb_ref,
                    o_ref, st_ref, dec_ref, kkn_ref, bb_ref, kh_ref, *, tb, lw):
    hd = RW_HEADDIM
    bh = r_ref.shape[-1]

    @pl.when(pl.program_id(0) == 0)
    def _():
        st_ref[...] = jnp.zeros_like(st_ref)

    def prep(t, carry):
        a = _sigmoid(ap_ref[t])
        kraw = k_ref[t]
        kk = kraw * kk_ref[...]
        nrm = jnp.sum(kk * kk, axis=0, keepdims=True)
        kkn = kk * lax.rsqrt(jnp.maximum(nrm, 1e-24))
        dec_ref[t] = jnp.exp(-jnp.exp(-_softplus(-wp_ref[t]) - 0.5))
        kkn_ref[t] = kkn
        bb_ref[t] = kkn * a
        kh_ref[t] = kraw * (1.0 + (a - 1.0) * ka_ref[...])
        return carry

    lax.fori_loop(0, tb, prep, 0)

    sub = 8

    def rows8(ref, t, kb, lanes):
        return ref[t, pl.ds(pl.multiple_of(kb * sub, sub), sub), lanes]

    def bcast(tile, j):
        return jnp.broadcast_to(tile[j:j + 1, :], (hd, lw))

    def step(t, carry):
        for c in range(bh // lw):
            lanes = slice(c * lw, (c + 1) * lw)

            def pass1(kb, sa):
                a8 = rows8(kkn_ref, t, kb, lanes)
                for j in range(sub):
                    sa = sa + st_ref[kb * sub + j, :, lanes] * bcast(a8, j)
                return sa

            sa = -lax.fori_loop(0, hd // sub, pass1, jnp.zeros((hd, lw), F32))
            vt = v_ref[t, :, lanes]

            def pass2(kb, y):
                d8 = rows8(dec_ref, t, kb, lanes)
                b8 = rows8(bb_ref, t, kb, lanes)
                k8 = rows8(kh_ref, t, kb, lanes)
                r8 = rows8(r_ref, t, kb, lanes)
                for j in range(sub):
                    s_new = (st_ref[kb * sub + j, :, lanes] * bcast(d8, j)
                             + sa * bcast(b8, j) + vt * bcast(k8, j))
                    st_ref[kb * sub + j, :, lanes] = s_new
                    y = y + s_new * bcast(r8, j)
                return y

            y = lax.fori_loop(0, hd // sub, pass2, jnp.zeros((hd, lw), F32))
            mu = jnp.mean(y, axis=0, keepdims=True)
            yc = y - mu
            var = jnp.mean(yc * yc, axis=0, keepdims=True)
            yn = yc * lax.rsqrt(var + RW_GN_EPS) * lnw_ref[:, lanes] + lnb_ref[:, lanes]
            bonus = jnp.sum(r_ref[t, :, lanes] * kh_ref[t, :, lanes] * rk_ref[:, lanes],
                            axis=0, keepdims=True)
            o_ref[t, :, lanes] = yn + bonus * vt
        return carry

    lax.fori_loop(0, tb, step, 0)


def rw_scan(r, k, v, wpre, apre, k_k, k_a, r_k, ln_w, ln_b):
    s, hd, bh = r.shape
    tb = _pick(s, (16, 8))
    lw = min(LANE, bh)
    seq = pl.BlockSpec((tb, hd, bh), lambda i: (i, 0, 0))
    par = pl.BlockSpec((hd, bh), lambda i: (0, 0))
    return pl.pallas_call(
        functools.partial(_rw_scan_kernel, tb=tb, lw=lw),
        grid=(s // tb,),
        in_specs=[seq] * 5 + [par] * 5,
        out_specs=seq,
        out_shape=jax.ShapeDtypeStruct((s, hd, bh), F32),
        scratch_shapes=[pltpu.VMEM((hd, hd, bh), F32)] + [pltpu.VMEM((tb, hd, bh), F32)] * 4,
        compiler_params=_cparams(("arbitrary",)),
        name="rw_scan",
    )(r, k, v, wpre, apre, k_k, k_a, r_k, ln_w, ln_b)


def rwkv7_layer(x, mod, mu, w_rkv, w0, w1, w2, a0, a1, a2, g1, g2, k_k, k_a, r_k, gn_w, gn_b,
                w_out, ln_w, ln_b, alpha):
    bsz, s, d = x.shape
    nh = d // RW_HEADDIM
    pad_c = lambda w: _pad_cols(w, RW_TN)
    pad_r = lambda w: jnp.pad(w, ((0, RW_TN - w.shape[0]), (0, 0)))
    secs = [w_rkv[0], w_rkv[1], w_rkv[2], pad_c(w1), pad_c(a1), pad_c(g1)]
    w_all = jnp.concatenate(secs, axis=1).astype(BF16)
    proj = rw_mix_mm(x, mod, mu, w_all)
    wpre, apre, gate = rw_lora2(proj, 3 * d, pad_r(w2).astype(BF16), pad_r(a2).astype(BF16),
                                pad_r(g2).astype(BF16), w0, a0)

    def to_scan(t):
        return t.reshape(bsz, s, nh, RW_HEADDIM).transpose(1, 3, 0, 2).reshape(s, RW_HEADDIM, bsz * nh)

    def par(p):
        return jnp.tile(p.reshape(nh, RW_HEADDIM).T[:, None, :], (1, bsz, 1)).reshape(RW_HEADDIM, bsz * nh)

    y = rw_scan(to_scan(proj[..., 0:d]), to_scan(proj[..., d:2 * d]), to_scan(proj[..., 2 * d:3 * d]),
                to_scan(wpre), to_scan(apre), par(k_k), par(k_a), par(r_k.reshape(-1)),
                par(gn_w), par(gn_b))
    y = y.reshape(s, RW_HEADDIM, bsz, nh).transpose(2, 0, 3, 1).reshape(bsz, s, d)
    return mm_res_ln(y, w_out.astype(BF16), x, mod, 2, ln_w, ln_b, alpha, h2=gate)


def kernel(x, c, ada_w, ada_b, ln_w, ln_b, hg_w_in, hg_lb_logits, hg_norm_w, hg_w_out, ssd_w_in, ssd_conv_w, ssd_conv_b, ssd_dt_bias, ssd_a_log, ssd_d, ssd_norm_w, ssd_w_out, da_w_in, da_w_out, rw_mu, rw_w_rkv, rw_w0, rw_w1, rw_w2, rw_a0, rw_a1, rw_a2, rw_g1, rw_g2, rw_k_k, rw_k_a, rw_r_k, rw_ln_w, rw_ln_b, rw_w_out, ffn_w_gu, ffn_w_down, moe_router, moe_w_gu, moe_w_down):
    depth = ada_w.shape[0]
    bsz, _, d = x.shape
    alpha = (2.0 * depth) ** 0.25
    lbs = lower_bounds(hg_lb_logits)
    mods = ada_modulation(c, ada_w, ada_b).reshape(depth, bsz, N_MOD, d)
    for i in range(depth):
        mixer, j = i % 4, i // 4
        mod = mods[i]
        if mixer == 0:
            x = hgrn2_layer(x, mod, hg_w_in[j], lbs[i], hg_norm_w[j], hg_w_out[j],
                            ln_w[i, 0], ln_b[i, 0], alpha)
        elif mixer == 1:
            x = mamba2_layer(x, mod, ssd_w_in[j], ssd_conv_w[j], ssd_conv_b[j], ssd_dt_bias[j],
                             ssd_a_log[j], ssd_d[j], ssd_norm_w[j], ssd_w_out[j],
                             ln_w[i, 0], ln_b[i, 0], alpha)
        elif mixer == 2:
            x = dilated_attention_layer(x, mod, da_w_in[j], da_w_out[j], ln_w[i, 0], ln_b[i, 0], alpha)
        else:
            x = rwkv7_layer(x, mod, rw_mu[j], rw_w_rkv[j], rw_w0[j], rw_w1[j], rw_w2[j], rw_a0[j],
                            rw_a1[j], rw_a2[j], rw_g1[j], rw_g2[j], rw_k_k[j], rw_k_a[j], rw_r_k[j],
                            rw_ln_w[j], rw_ln_b[j], rw_w_out[j], ln_w[i, 0], ln_b[i, 0], alpha)
        if i % 2 == 0:
            x = dense_ffn(x, mod, ffn_w_gu[i // 2], ffn_w_down[i // 2], ln_w[i, 1], ln_b[i, 1], alpha)
        else:
            x = moe_ffn(x, mod, moe_router[i // 2], moe_w_gu[i // 2], moe_w_down[i // 2],
                        ln_w[i, 1], ln_b[i, 1], alpha)
    return x
```

```python
import functools
import math

import numpy as np
import jax
import jax.numpy as jnp
from jax import lax
from jax.experimental import pallas as pl
from jax.experimental.pallas import tpu as pltpu

F32 = jnp.float32
BF16 = jnp.bfloat16

N_MOD = 6
LN_EPS = 1e-5
RMS_EPS = 1e-6

HG_DK = 128
GLA_CHUNK = 64
SSD_HEADDIM = 64
SSD_DSTATE = 128
SSD_CONV = 4
SSD_CHUNK = 256
DA_PATTERNS = ((128, 1), (512, 4), (2048, 16))
DA_HEADDIM = 128
DA_SPAN = 128
RW_HEADDIM = 64
RW_GN_EPS = 64e-5
MOE_TOPK = 2

LANE = 128
VMEM_LIMIT = 56 * 1024 * 1024


def _pick(n, cands):
    for c in cands:
        if n % c == 0:
            return c
    return n


def _cparams(sem):
    return pltpu.CompilerParams(dimension_semantics=sem, vmem_limit_bytes=VMEM_LIMIT)


def _sigmoid(x):
    return 1.0 / (1.0 + jnp.exp(-x))


def _silu(x):
    return x * _sigmoid(x)


def _softplus(x):
    return jnp.maximum(x, 0.0) + jnp.log(1.0 + jnp.exp(-jnp.abs(x)))


def _dot(a, b):
    return jnp.dot(a, b, preferred_element_type=F32)


def _dot_nt(a, b):
    return lax.dot_general(a, b, (((1,), (1,)), ((), ())), preferred_element_type=F32)


def _dot_tn(a, b):
    return lax.dot_general(a, b, (((0,), (0,)), ((), ())), preferred_element_type=F32)


def _ada_kernel(c_ref, w_ref, b_ref, o_ref):
    o_ref[0] = _dot(c_ref[...].astype(BF16), w_ref[0].astype(BF16)) + b_ref[0]


def ada_modulation(c, ada_w, ada_b):
    depth, d, n = ada_w.shape
    bsz = c.shape[0]
    tn = _pick(n, (1024, 512, 256, 128))
    return pl.pallas_call(
        _ada_kernel,
        grid=(depth, n // tn),
        in_specs=[
            pl.BlockSpec((bsz, d), lambda i, j: (0, 0)),
            pl.BlockSpec((1, d, tn), lambda i, j: (i, 0, j)),
            pl.BlockSpec((1, 1, tn), lambda i, j: (i, 0, j)),
        ],
        out_specs=pl.BlockSpec((1, bsz, tn), lambda i, j: (i, 0, j)),
        out_shape=jax.ShapeDtypeStruct((depth, bsz, n), F32),
        compiler_params=_cparams(("parallel", "parallel")),
        name="ada_modulation",
    )(c, ada_w, ada_b.reshape(depth, 1, n))


def _mm_mod_kernel(x_ref, mod_ref, w_ref, o_ref, h_scr, *, shift_row):
    @pl.when(pl.program_id(2) == 0)
    def _():
        shift = mod_ref[0, shift_row:shift_row + 1, :]
        scale = mod_ref[0, shift_row + 1:shift_row + 2, :]
        h_scr[...] = (x_ref[0] * (1.0 + scale) + shift).astype(BF16)

    o_ref[0] = _dot(h_scr[...], w_ref[...]).astype(o_ref.dtype)


def mm_mod(x, mod, w, shift_row, out_dtype=F32):
    bsz, s, d = x.shape
    n = w.shape[1]
    tm = _pick(s, (1024, 512, 256, 128))
    tn = _pick(n, (1024, 512, 256, 128))
    return pl.pallas_call(
        functools.partial(_mm_mod_kernel, shift_row=shift_row),
        grid=(bsz, s // tm, n // tn),
        in_specs=[
            pl.BlockSpec((1, tm, d), lambda b, i, j: (b, i, 0)),
            pl.BlockSpec((1, N_MOD, d), lambda b, i, j: (b, 0, 0)),
            pl.BlockSpec((d, tn), lambda b, i, j: (0, j)),
        ],
        out_specs=pl.BlockSpec((1, tm, tn), lambda b, i, j: (b, i, j)),
        out_shape=jax.ShapeDtypeStruct((bsz, s, n), out_dtype),
        scratch_shapes=[pltpu.VMEM((tm, d), BF16)],
        compiler_params=_cparams(("parallel", "parallel", "arbitrary")),
        name="mm_mod",
    )(x, mod, w)


def _mm_res_ln_kernel(*refs, gate_row, has_h2, alpha):
    if has_h2:
        h_ref, h2_ref, w_ref, x_ref, mod_ref, lnw_ref, lnb_ref, o_ref = refs
        h = (h_ref[0] * h2_ref[0]).astype(BF16)
    else:
        h_ref, w_ref, x_ref, mod_ref, lnw_ref, lnb_ref, o_ref = refs
        h = h_ref[0].astype(BF16)
    gate = mod_ref[0, gate_row:gate_row + 1, :]
    z = alpha * x_ref[0] + (1.0 + gate) * _dot(h, w_ref[...])
    mu = jnp.mean(z, axis=-1, keepdims=True)
    zc = z - mu
    var = jnp.mean(zc * zc, axis=-1, keepdims=True)
    o_ref[0] = zc * lax.rsqrt(var + LN_EPS) * lnw_ref[...] + lnb_ref[...]


def mm_res_ln(h, w, x, mod, gate_row, ln_w, ln_b, alpha, h2=None):
    bsz, s, d = x.shape
    kdim = h.shape[-1]
    tm = _pick(s, (512, 256, 128) if kdim <= 2 * d else (256, 128))
    h_spec = pl.BlockSpec((1, tm, kdim), lambda b, i: (b, i, 0))
    ins = [h] + ([h2] if h2 is not None else [])
    in_specs = [h_spec] * len(ins) + [
        pl.BlockSpec((kdim, d), lambda b, i: (0, 0), pipeline_mode=pl.Buffered(1)),
        pl.BlockSpec((1, tm, d), lambda b, i: (b, i, 0)),
        pl.BlockSpec((1, N_MOD, d), lambda b, i: (b, 0, 0)),
        pl.BlockSpec((1, d), lambda b, i: (0, 0)),
        pl.BlockSpec((1, d), lambda b, i: (0, 0)),
    ]
    return pl.pallas_call(
        functools.partial(_mm_res_ln_kernel, gate_row=gate_row, has_h2=h2 is not None, alpha=alpha),
        grid=(bsz, s // tm),
        in_specs=in_specs,
        out_specs=pl.BlockSpec((1, tm, d), lambda b, i: (b, i, 0)),
        out_shape=jax.ShapeDtypeStruct((bsz, s, d), F32),
        compiler_params=_cparams(("parallel", "parallel")),
        name="mm_res_ln",
    )(*ins, w, x, mod, ln_w.reshape(1, d), ln_b.reshape(1, d))


def _mm_swiglu_kernel(x_ref, mod_ref, wg_ref, wu_ref, o_ref, h_scr, *, shift_row):
    @pl.when(pl.program_id(2) == 0)
    def _():
        shift = mod_ref[0, shift_row:shift_row + 1, :]
        scale = mod_ref[0, shift_row + 1:shift_row + 2, :]
        h_scr[...] = (x_ref[0] * (1.0 + scale) + shift).astype(BF16)

    h = h_scr[...]
    g = _dot(h, wg_ref[...])
    u = _dot(h, wu_ref[...])
    o_ref[0] = (_silu(g) * u).astype(o_ref.dtype)


def mm_swiglu(x, mod, wg, wu, shift_row):
    bsz, s, d = x.shape
    f = wg.shape[1]
    tm = _pick(s, (1024, 512, 256, 128))
    tn = _pick(f, (512, 256, 128))
    return pl.pallas_call(
        functools.partial(_mm_swiglu_kernel, shift_row=shift_row),
        grid=(bsz, s // tm, f // tn),
        in_specs=[
            pl.BlockSpec((1, tm, d), lambda b, i, j: (b, i, 0)),
            pl.BlockSpec((1, N_MOD, d), lambda b, i, j: (b, 0, 0)),
            pl.BlockSpec((d, tn), lambda b, i, j: (0, j)),
            pl.BlockSpec((d, tn), lambda b, i, j: (0, j)),
        ],
        out_specs=pl.BlockSpec((1, tm, tn), lambda b, i, j: (b, i, j)),
        out_shape=jax.ShapeDtypeStruct((bsz, s, f), BF16),
        scratch_shapes=[pltpu.VMEM((tm, d), BF16)],
        compiler_params=_cparams(("parallel", "parallel", "arbitrary")),
        name="mm_swiglu",
    )(x, mod, wg, wu)


def _pad_cols(w, mult):
    n = w.shape[-1]
    pad = (-n) % mult
    return jnp.pad(w, ((0, 0), (0, pad))) if pad else w


def dense_ffn(x, mod, w_gu, w_down, ln_w, ln_b, alpha):
    f = w_gu.shape[1] // 2
    fmult = 512 if f >= 512 else LANE
    wg = _pad_cols(w_gu[:, :f], fmult).astype(BF16)
    wu = _pad_cols(w_gu[:, f:], fmult).astype(BF16)
    wd = jnp.pad(w_down, ((0, wg.shape[1] - f), (0, 0))).astype(BF16)
    hmid = mm_swiglu(x, mod, wg, wu, shift_row=3)
    return mm_res_ln(hmid, wd, x, mod, 5, ln_w, ln_b, alpha)


MOE_TM = 512
SEL_I1, SEL_I2, SEL_G1, SEL_G2, SEL_R1, SEL_R2 = range(6)


def _route_kernel(x_ref, mod_ref, wr_ref, ltri_ref, sel_ref, hm_ref, cnt_ref, cnt_scr, *, n_experts):
    @pl.when((pl.program_id(0) == 0) & (pl.program_id(1) == 0))
    def _():
        cnt_scr[...] = jnp.zeros_like(cnt_scr)

    shift = mod_ref[0, 3:4, :]
    scale = mod_ref[0, 4:5, :]
    h = x_ref[0] * (1.0 + scale) + shift
    hm_ref[0] = h
    logits = jnp.dot(h, wr_ref[...], preferred_element_type=F32, precision=lax.Precision.HIGHEST)
    lane = lax.broadcasted_iota(jnp.int32, logits.shape, 1)
    ninf = -jnp.inf
    lg = jnp.where(lane < n_experts, logits, ninf)
    m1 = jnp.max(lg, axis=-1, keepdims=True)
    i1 = jnp.min(jnp.where(lg == m1, lane, LANE), axis=-1, keepdims=True)
    lg2 = jnp.where(lane == i1, ninf, lg)
    m2 = jnp.max(lg2, axis=-1, keepdims=True)
    i2 = jnp.min(jnp.where(lg2 == m2, lane, LANE), axis=-1, keepdims=True)
    e2 = jnp.exp(m2 - m1)
    g1 = 1.0 / (1.0 + e2)
    hot1 = lane == i1
    hot2 = lane == i2
    chosen = jnp.where(hot1 | hot2, 1.0, 0.0)
    before = _dot(ltri_ref[...], chosen.astype(BF16)) + cnt_scr[...]
    r1 = jnp.sum(jnp.where(hot1, before, 0.0), axis=-1, keepdims=True)
    r2 = jnp.sum(jnp.where(hot2, before, 0.0), axis=-1, keepdims=True)
    cnt_scr[...] += jnp.sum(chosen, axis=0, keepdims=True)
    rec = jnp.zeros_like(logits)
    for slot, val in ((SEL_I1, i1.astype(F32)), (SEL_I2, i2.astype(F32)), (SEL_G1, g1),
                      (SEL_G2, e2 * g1), (SEL_R1, r1), (SEL_R2, r2)):
        rec = jnp.where(lane == slot, val, rec)
    sel_ref[0] = rec
    cnt_ref[...] = jnp.broadcast_to(cnt_scr[...], cnt_ref.shape)


def moe_route(x, mod, w_router):
    bsz, s, d = x.shape
    n_experts = w_router.shape[1]
    tm = _pick(s, (256, 128))
    ltri = jnp.asarray(np.tril(np.ones((tm, tm), np.float32), -1), BF16)
    return pl.pallas_call(
        functools.partial(_route_kernel, n_experts=n_experts),
        grid=(bsz, s // tm),
        in_specs=[
            pl.BlockSpec((1, tm, d), lambda b, i: (b, i, 0)),
            pl.BlockSpec((1, N_MOD, d), lambda b, i: (b, 0, 0)),
            pl.BlockSpec((d, LANE), lambda b, i: (0, 0)),
            pl.BlockSpec((tm, tm), lambda b, i: (0, 0)),
        ],
        out_specs=[
            pl.BlockSpec((1, tm, LANE), lambda b, i: (b, i, 0)),
            pl.BlockSpec((1, tm, d), lambda b, i: (b, i, 0)),
            pl.BlockSpec((8, LANE), lambda b, i: (0, 0)),
        ],
        out_shape=[
            jax.ShapeDtypeStruct((bsz, s, LANE), F32),
            jax.ShapeDtypeStruct((bsz, s, d), F32),
            jax.ShapeDtypeStruct((8, LANE), F32),
        ],
        scratch_shapes=[pltpu.VMEM((1, LANE), F32)],
        compiler_params=_cparams(("arbitrary", "arbitrary")),
        name="moe_route",
    )(x, mod, _pad_cols(w_router, LANE), ltri)


def _row_copy(src_ref, src_row, dst_ref, dst_row, sem):
    return pltpu.make_async_copy(src_ref.at[pl.ds(src_row, 1)], dst_ref.at[pl.ds(dst_row, 1)], sem)


def _dispatch_kernel(pos_ref, hm_ref, zero_ref, hs_ref, sem, *, tm):
    del zero_ref
    src = hm_ref.at[0]

    def start(r, c):
        for k in range(MOE_TOPK):
            _row_copy(src, r, hs_ref, pos_ref[0, 0, MOE_TOPK * r + k], sem).start()
        return c

    def wait(r, c):
        for k in range(MOE_TOPK):
            _row_copy(src, r, hs_ref, pos_ref[0, 0, MOE_TOPK * r + k], sem).wait()
        return c

    lax.fori_loop(0, tm, start, 0)
    lax.fori_loop(0, tm, wait, 0)


def moe_dispatch(hm, pos, n_rows):
    bsz, s, d = hm.shape
    tm = _pick(s, (256, 128))
    nt = s // tm
    pos_t = pos.reshape(bsz * nt, 1, MOE_TOPK * tm)
    return pl.pallas_call(
        functools.partial(_dispatch_kernel, tm=tm),
        grid=(bsz, nt),
        in_specs=[
            pl.BlockSpec((1, 1, MOE_TOPK * tm), lambda b, i: (b * nt + i, 0, 0), memory_space=pltpu.SMEM),
            pl.BlockSpec((1, tm, d), lambda b, i: (b, i, 0)),
            pl.BlockSpec(memory_space=pl.ANY),
        ],
        out_specs=pl.BlockSpec(memory_space=pl.ANY),
        out_shape=jax.ShapeDtypeStruct((n_rows, d), F32),
        scratch_shapes=[pltpu.SemaphoreType.DMA(())],
        input_output_aliases={2: 0},
        compiler_params=_cparams(("arbitrary", "arbitrary")),
        name="moe_dispatch",
    )(pos_t, hm, jnp.zeros((n_rows, d), F32))


def _moe_gu_kernel(te_ref, nu_ref, hs_ref, wg_ref, wu_ref, o_ref, h_scr):
    del te_ref

    @pl.when(pl.program_id(0) < nu_ref[0])
    def _():
        @pl.when(pl.program_id(1) == 0)
        def _():
            h_scr[...] = hs_ref[...].astype(BF16)

        h = h_scr[...]
        g = _dot(h, wg_ref[0])
        u = _dot(h, wu_ref[0])
        o_ref[...] = (_silu(g) * u).astype(o_ref.dtype)

    @pl.when(pl.program_id(0) >= nu_ref[0])
    def _():
        o_ref[...] = jnp.zeros_like(o_ref)


def _moe_down_kernel(te_ref, nu_ref, h_ref, w_ref, o_ref):
    del te_ref

    @pl.when(pl.program_id(1) < nu_ref[0])
    def _():
        o_ref[...] = _dot(h_ref[...], w_ref[0])

    @pl.when(pl.program_id(1) >= nu_ref[0])
    def _():
        o_ref[...] = jnp.zeros_like(o_ref)


def _moe_combine_kernel(pos_ref, ys_ref, sel_ref, x_ref, mod_ref, lnw_ref, lnb_ref, o_ref, buf, sem, *,
                        tm, alpha):
    def start(r, c):
        for k in range(MOE_TOPK):
            _row_copy(ys_ref, pos_ref[0, 0, MOE_TOPK * r + k], buf.at[k], r, sem).start()
        return c

    def wait(r, c):
        for k in range(MOE_TOPK):
            _row_copy(ys_ref, pos_ref[0, 0, MOE_TOPK * r + k], buf.at[k], r, sem).wait()
        return c

    lax.fori_loop(0, tm, start, 0)
    lax.fori_loop(0, tm, wait, 0)
    sel = sel_ref[0]
    lane = lax.broadcasted_iota(jnp.int32, sel.shape, 1)
    g1 = jnp.sum(jnp.where(lane == SEL_G1, sel, 0.0), axis=-1, keepdims=True)
    g2 = jnp.sum(jnp.where(lane == SEL_G2, sel, 0.0), axis=-1, keepdims=True)
    y = g1 * buf[0] + g2 * buf[1]
    z = alpha * x_ref[0] + (1.0 + mod_ref[0, 5:6, :]) * y
    mu = jnp.mean(z, axis=-1, keepdims=True)
    zc = z - mu
    var = jnp.mean(zc * zc, axis=-1, keepdims=True)
    o_ref[0] = zc * lax.rsqrt(var + LN_EPS) * lnw_ref[...] + lnb_ref[...]


def moe_ffn(x, mod, w_router, w_gu, w_down, e0, ln_w, ln_b, alpha):
    bsz, s, d = x.shape
    n_experts = w_router.shape[1]
    ff = w_gu.shape[2] // 2
    tmr = min(MOE_TM, s)
    n_rows = MOE_TOPK * bsz * s + n_experts * tmr
    n_tiles = n_rows // tmr

    sel, hm, cnt = moe_route(x, mod, w_router)

    counts = cnt[0, :n_experts].astype(jnp.int32)
    padded = (counts + tmr - 1) // tmr * tmr
    ends = jnp.cumsum(padded)
    offs = ends - padded
    tile_expert = jnp.minimum(
        jnp.searchsorted(ends, jnp.arange(n_tiles, dtype=jnp.int32) * tmr, side="right"),
        n_experts - 1).astype(jnp.int32)
    n_used = (ends[-1:] // tmr).astype(jnp.int32)
    ids = sel[..., SEL_I1:SEL_I2 + 1].astype(jnp.int32)
    ranks = sel[..., SEL_R1:SEL_R2 + 1].astype(jnp.int32)
    hot = ids[..., None] == jnp.arange(n_experts, dtype=jnp.int32)
    pos = jnp.sum(jnp.where(hot, offs, 0), axis=-1) + ranks

    hs = moe_dispatch(hm, pos, n_rows)

    tn = _pick(ff, (1024, 512, 256, 128))
    nt = ff // tn
    expert = lambda i, te, nu: e0 + te[live(i, nu)]
    live = lambda i, nu: jnp.minimum(i, nu[0] - 1)
    col = lambda i, j, nu: jnp.where(i < nu[0], j, nt - 1)
    hmid = pl.pallas_call(
        _moe_gu_kernel,
        grid_spec=pltpu.PrefetchScalarGridSpec(
            num_scalar_prefetch=2,
            grid=(n_tiles, nt),
            in_specs=[
                pl.BlockSpec((tmr, d), lambda i, j, te, nu: (live(i, nu), 0)),
                pl.BlockSpec((1, d, tn), lambda i, j, te, nu: (expert(i, te, nu), 0, col(i, j, nu))),
                pl.BlockSpec((1, d, tn), lambda i, j, te, nu: (expert(i, te, nu), 0, nt + col(i, j, nu))),
            ],
            out_specs=pl.BlockSpec((tmr, tn), lambda i, j, te, nu: (i, j)),
            scratch_shapes=[pltpu.VMEM((tmr, d), BF16)],
        ),
        out_shape=jax.ShapeDtypeStruct((n_rows, ff), BF16),
        compiler_params=_cparams(("arbitrary", "arbitrary")),
        name="moe_gate_up",
    )(tile_expert, n_used, hs, w_gu, w_gu)

    dn = _pick(d, (1024, 512, 256, 128))
    ys = pl.pallas_call(
        _moe_down_kernel,
        grid_spec=pltpu.PrefetchScalarGridSpec(
            num_scalar_prefetch=2,
            grid=(d // dn, n_tiles),
            in_specs=[
                pl.BlockSpec((tmr, ff), lambda n, i, te, nu: (live(i, nu), 0)),
                pl.BlockSpec((1, ff, dn), lambda n, i, te, nu: (expert(i, te, nu), 0, n)),
            ],
            out_specs=pl.BlockSpec((tmr, dn), lambda n, i, te, nu: (i, n)),
        ),
        out_shape=jax.ShapeDtypeStruct((n_rows, d), F32),
        compiler_params=_cparams(("arbitrary", "arbitrary")),
        name="moe_down",
    )(tile_expert, n_used, hmid, w_down)

    tm = _pick(s, (256, 128))
    ntk = s // tm
    return pl.pallas_call(
        functools.partial(_moe_combine_kernel, tm=tm, alpha=alpha),
        grid=(bsz, ntk),
        in_specs=[
            pl.BlockSpec((1, 1, MOE_TOPK * tm), lambda b, i: (b * ntk + i, 0, 0), memory_space=pltpu.SMEM),
            pl.BlockSpec(memory_space=pl.ANY),
            pl.BlockSpec((1, tm, LANE), lambda b, i: (b, i, 0)),
            pl.BlockSpec((1, tm, d), lambda b, i: (b, i, 0)),
            pl.BlockSpec((1, N_MOD, d), lambda b, i: (b, 0, 0)),
            pl.BlockSpec((1, d), lambda b, i: (0, 0)),
            pl.BlockSpec((1, d), lambda b, i: (0, 0)),
        ],
        out_specs=pl.BlockSpec((1, tm, d), lambda b, i: (b, i, 0)),
        out_shape=jax.ShapeDtypeStruct((bsz, s, d), F32),
        scratch_shapes=[pltpu.VMEM((MOE_TOPK, tm, d), F32), pltpu.SemaphoreType.DMA(())],
        compiler_params=_cparams(("arbitrary", "arbitrary")),
        name="moe_combine",
    )(pos.reshape(bsz * ntk, 1, MOE_TOPK * tm), ys, sel, x, mod, ln_w.reshape(1, d), ln_b.reshape(1, d))


def _gla_tables(chunk):
    t = np.arange(chunk)[:, None]
    u = np.arange(chunk)[None, :]
    mats = [(u <= t), (u > t)]
    levels = int(math.log2(chunk))
    for l in range(1, levels + 1):
        half = 1 << (l - 1)
        anchor = ((t >> l) << l) + half - 1
        upper = (t & half) != 0
        mats.append(np.where(upper, (u > anchor) & (u <= t), (u > t) & (u <= anchor)))
    return np.concatenate(mats, axis=0).astype(np.float32), levels


def _gla_kernel(q_ref, f_ref, i_ref, g_ref, lb_ref, nw_ref, m_ref, o_ref, st_ref, *,
                chunk, levels, n_chunks):
    @pl.when(pl.program_id(2) == 0)
    def _():
        st_ref[...] = jnp.zeros_like(st_ref)

    lb = lb_ref[...]
    nw = nw_ref[...]
    row = lax.broadcasted_iota(jnp.int32, (chunk, HG_DK), 0)
    srow = lax.broadcasted_iota(jnp.int32, (chunk, chunk), 0)
    scol = lax.broadcasted_iota(jnp.int32, (chunk, chunk), 1)

    chunks = range(n_chunks)
    qq, kk, vv, e = [], [], [], []
    for c in chunks:
        rows = slice(c * chunk, (c + 1) * chunk)
        fg = lb + (1.0 - lb) * _sigmoid(f_ref[0, rows, :])
        logf = jnp.log(fg)
        kk.append(1.0 - fg)
        qq.append(_silu(q_ref[0, rows, :]))
        vv.append(i_ref[0, rows, :])
        g_hi = logf.astype(BF16)
        g_lo = (logf - g_hi.astype(F32)).astype(BF16)
        e2 = _dot(m_ref[...], jnp.concatenate([g_hi, g_lo], axis=1))
        e.append(e2[:, :HG_DK] + e2[:, HG_DK:])

    scores = [None] * n_chunks
    for l in range(1, levels + 1):
        half = 1 << (l - 1)
        upper = (row & half) != 0
        same_block = (srow >> l) == (scol >> l)
        for c in chunks:
            a = jnp.exp(e[c][(l + 1) * chunk:(l + 2) * chunk])
            qa = jnp.where(upper, qq[c] * a, 0.0).astype(BF16)
            ka = jnp.where(upper, 0.0, kk[c] * a).astype(BF16)
            s_l = _dot_nt(qa, ka)
            if l < levels:
                s_l = jnp.where(same_block, s_l, 0.0)
            scores[c] = s_l if scores[c] is None else scores[c] + s_l

    pending = []
    for c in chunks:
        b = e[c][0:chunk]
        b_rest = e[c][chunk:2 * chunk]
        vb = vv[c].astype(BF16)
        o = jnp.sum(qq[c] * kk[c], axis=-1, keepdims=True) * vv[c] + _dot(scores[c].astype(BF16), vb)
        q_dec = (qq[c] * jnp.exp(b)).astype(BF16)
        st_add = _dot_tn(vb, (kk[c] * jnp.exp(b_rest)).astype(BF16))
        st_dec = jnp.exp(b[chunk - 1:chunk, :])
        pending.append((o, q_dec, st_add, st_dec))

    st = st_ref[...]
    for c in range(n_chunks):
        rows = slice(c * chunk, (c + 1) * chunk)
        o, q_dec, st_add, st_dec = pending[c]
        o = o + _dot_nt(q_dec, st.astype(BF16))
        st = st * st_dec + st_add
        o = o * lax.rsqrt(jnp.mean(o * o, axis=-1, keepdims=True) + RMS_EPS) * nw
        o_ref[0, rows, :] = (o * _silu(g_ref[0, rows, :])).astype(o_ref.dtype)
    st_ref[...] = st


def hgrn2_core(proj, lb, norm_w):
    bsz, s, d4 = proj.shape
    d = d4 // 4
    nh = d // HG_DK
    tt = _pick(s, (512, 256, 128, 64))
    chunk = min(GLA_CHUNK, tt)
    tables, levels = _gla_tables(chunk)
    col = lambda part: (lambda b, h, t: (b, t, part * nh + h))
    blk = (1, tt, HG_DK)
    return pl.pallas_call(
        functools.partial(_gla_kernel, chunk=chunk, levels=levels, n_chunks=tt // chunk),
        grid=(bsz, nh, s // tt),
        in_specs=[
            pl.BlockSpec(blk, col(0)),
            pl.BlockSpec(blk, col(1)),
            pl.BlockSpec(blk, col(2)),
            pl.BlockSpec(blk, col(3)),
            pl.BlockSpec((1, HG_DK), lambda b, h, t: (0, h)),
            pl.BlockSpec((1, HG_DK), lambda b, h, t: (0, 0)),
            pl.BlockSpec(tables.shape, lambda b, h, t: (0, 0)),
        ],
        out_specs=pl.BlockSpec(blk, lambda b, h, t: (b, t, h)),
        out_shape=jax.ShapeDtypeStruct((bsz, s, d), BF16),
        scratch_shapes=[pltpu.VMEM((HG_DK, HG_DK), F32)],
        compiler_params=_cparams(("parallel", "parallel", "arbitrary")),
        name="hgrn2_core",
    )(proj, proj, proj, proj, lb.reshape(1, d), norm_w.reshape(1, HG_DK),
      jnp.asarray(tables, BF16))


def _lower_bounds_kernel(l_ref, o_ref):
    x = l_ref[...]
    e = jnp.exp(x - jnp.max(x, axis=0, keepdims=True))
    p = e / jnp.sum(e, axis=0, keepdims=True)
    n = x.shape[0]
    r = lax.broadcasted_iota(jnp.int32, (n, n), 0)
    c = lax.broadcasted_iota(jnp.int32, (n, n), 1)
    acc = jnp.zeros_like(p)
    for j in range(n):
        acc = acc + jnp.where(r[:, j:j + 1] >= j, 1.0, 0.0) * p[j:j + 1, :]
    del c
    o_ref[...] = acc


def lower_bounds(logits):
    return pl.pallas_call(
        _lower_bounds_kernel,
        out_shape=jax.ShapeDtypeStruct(logits.shape, F32),
        name="hgrn2_lower_bounds",
    )(logits)


def hgrn2_layer(x, mod, w_in, lb, norm_w, w_out, ln_w, ln_b, alpha):
    proj = mm_mod(x, mod, w_in.astype(BF16), shift_row=0)
    o = hgrn2_core(proj, lb, norm_w)
    return mm_res_ln(o, w_out.astype(BF16), x, mod, 2, ln_w, ln_b, alpha)


def _split_hi_lo(v):
    hi = v.astype(BF16)
    lo = (v - hi.astype(F32)).astype(BF16)
    return hi, lo


def _ssd_kernel(xbc_ref, z_ref, dt_ref, cw_ref, cb_ref, dtb_ref, alog_ref, dsk_ref, nw_ref,
                ltri_ref, exp_ref, o_ref, st_ref, ext_ref, xc_ref, *, tc, di, groups):
    gw = di // groups
    gn = groups * SSD_DSTATE
    ch = di + 2 * gn
    halo = 8

    @pl.when(pl.program_id(1) == 0)
    def _():
        st_ref[...] = jnp.zeros_like(st_ref)
        ext_ref[0:halo, :] = jnp.zeros((halo, ch), F32)

    ext_ref[halo:halo + tc, :] = xbc_ref[0]
    cblk = _pick(ch, (512, 256, 128))
    for j in range(ch // cblk):
        cs = slice(j * cblk, (j + 1) * cblk)
        acc = cb_ref[:, cs] + cw_ref[0:1, cs] * ext_ref[halo - 3:halo - 3 + tc, cs]
        for t in range(1, SSD_CONV):
            acc = acc + cw_ref[t:t + 1, cs] * ext_ref[halo - 3 + t:halo - 3 + t + tc, cs]
        xc_ref[:, cs] = _silu(acc)
    ext_ref[0:halo, :] = ext_ref[tc:tc + halo, :]

    dt = _softplus(dt_ref[0] + dtb_ref[...])
    da = dt * (-jnp.exp(alog_ref[...]))
    da_hi, da_lo = _split_hi_lo(da)
    a2 = _dot(ltri_ref[...], jnp.concatenate([da_hi, da_lo], axis=1))
    a = a2[:, :LANE] + a2[:, LANE:]
    a_t = a.T
    a_end = a[tc - 1:tc, :]
    a_hi, a_lo = _split_hi_lo(a)
    d_hi, d_lo = _split_hi_lo(dt)
    e_hi, e_lo = _split_hi_lo(a_end)
    stack = jnp.concatenate([a_hi, a_lo, d_hi, d_lo,
                             jnp.broadcast_to(e_hi, (8, LANE)), jnp.broadcast_to(e_lo, (8, LANE))], axis=0)
    row = lax.broadcasted_iota(jnp.int32, (tc, tc), 0)
    col = lax.broadcasted_iota(jnp.int32, (tc, tc), 1)
    causal = row >= col
    lane = lax.broadcasted_iota(jnp.int32, (tc, LANE), 1)
    low_half = lane < SSD_HEADDIM

    for g in range(groups):
        gs = slice(g * gw, (g + 1) * gw)
        ex = _dot(stack, exp_ref[:, gs])
        a_x = ex[0:tc] + ex[tc:2 * tc]
        dt_x = ex[2 * tc:3 * tc] + ex[3 * tc:4 * tc]
        ae_x = ex[4 * tc:4 * tc + 1] + ex[4 * tc + 8:4 * tc + 9]
        bg = xc_ref[:, di + g * SSD_DSTATE:di + (g + 1) * SSD_DSTATE].astype(BF16)
        cg = xc_ref[:, di + gn + g * SSD_DSTATE:di + gn + (g + 1) * SSD_DSTATE].astype(BF16)
        xs = xc_ref[:, gs]
        xdt = xs * dt_x
        cbm = _dot_nt(cg, bg)
        st = st_ref[g * SSD_DSTATE:(g + 1) * SSD_DSTATE, :]
        y = jnp.exp(a_x) * _dot(cg, st.astype(BF16)) + dsk_ref[:, gs] * xs
        tiles = []
        for j in range(gw // LANE):
            xt = xdt[:, j * LANE:(j + 1) * LANE]
            acc = None
            for half in range(2):
                h = (g * gw + j * LANE) // SSD_HEADDIM + half
                seg = a[:, h:h + 1] - a_t[h:h + 1, :]
                m = (cbm * jnp.exp(jnp.where(causal, seg, -1e30))).astype(BF16)
                xm = jnp.where(low_half if half == 0 else jnp.logical_not(low_half), xt, 0.0)
                part = _dot(m, xm.astype(BF16))
                acc = part if acc is None else acc + part
            tiles.append(acc)
        y = y + jnp.concatenate(tiles, axis=1)
        xw = (xdt * jnp.exp(ae_x - a_x)).astype(BF16)
        st_ref[g * SSD_DSTATE:(g + 1) * SSD_DSTATE, :] = st * jnp.exp(ae_x) + _dot_tn(bg, xw)
        y = y * _silu(z_ref[0, :, gs])
        y = y * lax.rsqrt(jnp.mean(y * y, axis=-1, keepdims=True) + RMS_EPS) * nw_ref[:, gs]
        o_ref[0, :, gs] = y.astype(o_ref.dtype)


def ssd_core(xbc, z, dt, conv_w, conv_b, dt_bias, a_log, d_skip, norm_w):
    bsz, s, ch = xbc.shape
    di = z.shape[-1]
    nh = di // SSD_HEADDIM
    groups = (ch - di) // (2 * SSD_DSTATE)
    gw = di // groups
    tc = min(SSD_CHUNK, s)
    pad_h = lambda v: jnp.pad(v, (0, LANE - nh)).reshape(1, LANE)
    ltri = jnp.asarray(np.tril(np.ones((tc, tc), np.float32)), BF16)
    expand = np.zeros((LANE, di), np.float32)
    expand[np.arange(di) // SSD_HEADDIM, np.arange(di)] = 1.0
    full = lambda shape: pl.BlockSpec(shape, lambda b, c: (0,) * len(shape))
    return pl.pallas_call(
        functools.partial(_ssd_kernel, tc=tc, di=di, groups=groups),
        grid=(bsz, s // tc),
        in_specs=[
            pl.BlockSpec((1, tc, ch), lambda b, c: (b, c, 0)),
            pl.BlockSpec((1, tc, di), lambda b, c: (b, c, 0)),
            pl.BlockSpec((1, tc, LANE), lambda b, c: (b, c, 0)),
            full((SSD_CONV, ch)), full((1, ch)), full((1, LANE)), full((1, LANE)),
            full((1, di)), full((1, di)), full((tc, tc)), full((LANE, di)),
        ],
        out_specs=pl.BlockSpec((1, tc, di), lambda b, c: (b, c, 0)),
        out_shape=jax.ShapeDtypeStruct((bsz, s, di), BF16),
        scratch_shapes=[
            pltpu.VMEM((groups * SSD_DSTATE, gw), F32),
            pltpu.VMEM((tc + 8, ch), F32),
            pltpu.VMEM((tc, ch), F32),
        ],
        compiler_params=_cparams(("parallel", "arbitrary")),
        name="ssd_core",
    )(xbc, z, dt, conv_w, conv_b.reshape(1, ch), pad_h(dt_bias), pad_h(a_log),
      jnp.repeat(d_skip, SSD_HEADDIM).reshape(1, di), norm_w.reshape(1, di), ltri,
      jnp.asarray(expand, BF16))


def mamba2_layer(x, mod, w_in, conv_w, conv_b, dt_bias, a_log, d_skip, norm_w, w_out,
                 ln_w, ln_b, alpha):
    di = w_out.shape[0]
    ch = conv_w.shape[1]
    wb = w_in.astype(BF16)
    z = mm_mod(x, mod, wb[:, :di], shift_row=0)
    xbc = mm_mod(x, mod, wb[:, di:di + ch], shift_row=0)
    dt = mm_mod(x, mod, _pad_cols(wb[:, di + ch:], LANE), shift_row=0)
    y = ssd_core(xbc, z, dt, conv_w, conv_b, dt_bias, a_log, d_skip, norm_w)
    return mm_res_ln(y, w_out.astype(BF16), x, mod, 2, ln_w, ln_b, alpha)


DA_BATCH = 4


def _rows(start, size, stride):
    return pl.ds(start, size) if stride == 1 else pl.ds(start, size, stride=stride)


def _da_kernel(*refs, tile, n_heads, scale):
    n_grp = len(DA_PATTERNS)
    ins = refs[:5 * n_grp]
    o_ref, m_ref, l_ref, acc_ref = refs[5 * n_grp:]
    span = DA_SPAN
    first_tile = pl.program_id(2) == 0
    head = jnp.full((1, 2 * span), pl.program_id(1), jnp.int32).astype(F32)
    slope = jnp.exp((head + 1.0) * (-8.0 * math.log(2.0) / n_heads))
    qi = lax.broadcasted_iota(jnp.int32, (span, 2 * span), 0)
    ki = lax.broadcasted_iota(jnp.int32, (span, 2 * span), 1)
    dist = qi + span - ki
    valid = (dist >= 0) & (dist <= span)
    valid_first = valid & jnp.logical_or(ki >= span, jnp.logical_not(first_tile))
    neg = -1e30

    for g, (window, dil) in enumerate(DA_PATTERNS):
        q_ref, kc_ref, vc_ref, kp_ref, vp_ref = ins[5 * g:5 * g + 5]
        bias = dist.astype(F32) * (slope * (-float(dil)))
        blocks = [(r, i) for r in range(dil) for i in range(tile // (span * dil))]
        for b0 in range(0, len(blocks), DA_BATCH):
            batch = blocks[b0:b0 + DA_BATCH]
            qrows, vs, ss = [], [], []
            for r, i in batch:
                rows = _rows(i * span * dil + r, span, dil)
                q = (q_ref[0, rows, :] * scale).astype(BF16)
                if i == 0:
                    prow = _rows(r, span, dil)
                    k = jnp.concatenate([kp_ref[0, prow, :], kc_ref[0, prow, :]], axis=0)
                    v = jnp.concatenate([vp_ref[0, prow, :], vc_ref[0, prow, :]], axis=0)
                    ok = valid_first
                else:
                    krows = _rows((i - 1) * span * dil + r, 2 * span, dil)
                    k = kc_ref[0, krows, :]
                    v = vc_ref[0, krows, :]
                    ok = valid
                qrows.append(rows)
                vs.append(v.astype(BF16))
                ss.append(jnp.where(ok, _dot_nt(q, k.astype(BF16)) + bias, neg))
            stats = []
            for rows, s in zip(qrows, ss):
                m_blk = jnp.max(s, axis=-1, keepdims=True)
                if g == 0:
                    m_new, corr = m_blk, None
                else:
                    m_old = m_ref[rows, :]
                    m_new = jnp.maximum(m_old, m_blk)
                    corr = jnp.exp(m_old - m_new)
                p = jnp.exp(s - m_new)
                stats.append((m_new, corr, jnp.sum(p, axis=-1, keepdims=True), p.astype(BF16)))
            pvs = [_dot(st[3], v) for st, v in zip(stats, vs)]
            for rows, (m_new, corr, l_blk, _), pv in zip(qrows, stats, pvs):
                if g == 0:
                    l_new, acc_new = l_blk, pv
                else:
                    l_new = corr * l_ref[rows, :] + l_blk
                    acc_new = corr * acc_ref[rows, :] + pv
                m_ref[rows, :] = m_new
                l_ref[rows, :] = l_new
                acc_ref[rows, :] = acc_new

    o_ref[0] = (acc_ref[...] / l_ref[...]).astype(o_ref.dtype)


def dilated_attention_core(qkv, n_heads):
    bsz, s, _ = qkv.shape
    hd = DA_HEADDIM
    tile = max(w for w, _ in DA_PATTERNS)
    assert s % tile == 0 and all(w // d == DA_SPAN for w, d in DA_PATTERNS)
    in_specs, args = [], []
    for g, (window, dil) in enumerate(DA_PATTERNS):
        prev_rows = DA_SPAN * dil
        per_tile = tile // prev_rows
        for part in (0, 1, 2):
            colb = (g * 3 + part) * n_heads
            in_specs.append(pl.BlockSpec((1, tile, hd), lambda b, h, t, colb=colb: (b, t, colb + h)))
            args.append(qkv)
            if part == 0:
                continue
        for part in (1, 2):
            colb = (g * 3 + part) * n_heads
            in_specs.append(pl.BlockSpec(
                (1, prev_rows, hd),
                lambda b, h, t, colb=colb, per_tile=per_tile: (b, jnp.maximum(t * per_tile - 1, 0), colb + h)))
            args.append(qkv)
    return pl.pallas_call(
        functools.partial(_da_kernel, tile=tile, n_heads=n_heads, scale=hd ** -0.5),
        grid=(bsz, n_heads, s // tile),
        in_specs=in_specs,
        out_specs=pl.BlockSpec((1, tile, hd), lambda b, h, t: (b, t, h)),
        out_shape=jax.ShapeDtypeStruct((bsz, s, n_heads * hd), BF16),
        scratch_shapes=[
            pltpu.VMEM((tile, 1), F32),
            pltpu.VMEM((tile, 1), F32),
            pltpu.VMEM((tile, hd), F32),
        ],
        compiler_params=_cparams(("parallel", "parallel", "arbitrary")),
        name="dilated_attention",
    )(*args)


def dilated_attention_layer(x, mod, w_in, w_out, ln_w, ln_b, alpha):
    d = x.shape[-1]
    qkv = mm_mod(x, mod, w_in.astype(BF16), shift_row=0)
    o = dilated_attention_core(qkv, d // DA_HEADDIM)
    return mm_res_ln(o, w_out.astype(BF16), x, mod, 2, ln_w, ln_b, alpha)


RW_TN = 256


RW_N_MIX = 6
RW_SLAB = 256


def _rw_mix_kernel(x_ref, xp_ref, mod_ref, mu_ref, w_ref, rkv_ref, lora_ref, mix_scr, *, tiles_per_proj):
    j = pl.program_id(2)

    @pl.when(j == 0)
    def _():
        shift = mod_ref[0, 0:1, :]
        scale = 1.0 + mod_ref[0, 1:2, :]
        tm = x_ref.shape[1]
        slab = min(RW_SLAB, tm)
        row = lax.broadcasted_iota(jnp.int32, (slab, x_ref.shape[2]), 0)
        for s0 in range(0, tm, slab):
            h = x_ref[0, s0:s0 + slab, :] * scale + shift
            if s0 == 0:
                prev_row = xp_ref[0, 7:8, :] * scale + shift
                prev_row = jnp.where(pl.program_id(1) == 0, 0.0, prev_row)
            else:
                prev_row = x_ref[0, s0 - 1:s0, :] * scale + shift
            xx = jnp.where(row == 0, prev_row, pltpu.roll(h, 1, 0)) - h
            for m in range(RW_N_MIX):
                mix_scr[m, s0:s0 + slab, :] = (h + xx * mu_ref[m:m + 1, :]).astype(BF16)

    n_wide = 3 * tiles_per_proj
    mix = jnp.where(j < n_wide, j // tiles_per_proj, 3 + j - n_wide)
    res = _dot(mix_scr[mix], w_ref[...])

    @pl.when(j < n_wide)
    def _():
        rkv_ref[0, 0] = res

    @pl.when(j >= n_wide)
    def _():
        lora_ref[0] = res


def rw_mix_mm(x, mod, mu, w_all):
    bsz, s, d = x.shape
    n = w_all.shape[1]
    tm = _pick(s, (1024, 512, 256, 128))
    nd = d // RW_TN
    n_wide = 3 * nd
    return pl.pallas_call(
        functools.partial(_rw_mix_kernel, tiles_per_proj=nd),
        grid=(bsz, s // tm, n // RW_TN),
        in_specs=[
            pl.BlockSpec((1, tm, d), lambda b, i, j: (b, i, 0)),
            pl.BlockSpec((1, 8, d), lambda b, i, j: (b, jnp.maximum(i * (tm // 8) - 1, 0), 0)),
            pl.BlockSpec((1, N_MOD, d), lambda b, i, j: (b, 0, 0)),
            pl.BlockSpec((RW_N_MIX, d), lambda b, i, j: (0, 0)),
            pl.BlockSpec((d, RW_TN), lambda b, i, j: (0, j)),
        ],
        out_specs=[
            pl.BlockSpec((1, 1, tm, RW_TN),
                         lambda b, i, j: (jnp.minimum(j // nd, 2), b, i, jnp.where(j < n_wide, j % nd, nd - 1))),
            pl.BlockSpec((1, tm, RW_TN), lambda b, i, j: (b, i, jnp.maximum(j - n_wide, 0))),
        ],
        out_shape=[
            jax.ShapeDtypeStruct((3, bsz, s, d), F32),
            jax.ShapeDtypeStruct((bsz, s, n - 3 * d), F32),
        ],
        scratch_shapes=[pltpu.VMEM((RW_N_MIX, tm, d), BF16)],
        compiler_params=_cparams(("parallel", "parallel", "arbitrary")),
        name="rw_mix_mm",
    )(x, x, mod, mu, w_all)


def _rw_lora2_kernel(w1h_ref, a1h_ref, g1h_ref, w2_ref, a2_ref, g2_ref, w0_ref, a0_ref,
                     wp_ref, ap_ref, gt_ref):
    wp_ref[0] = w0_ref[...] + _dot(jnp.tanh(w1h_ref[0]).astype(BF16), w2_ref[...])
    ap_ref[0] = a0_ref[...] + _dot(a1h_ref[0].astype(BF16), a2_ref[...])
    gt_ref[0] = _dot(_sigmoid(g1h_ref[0]).astype(BF16), g2_ref[...])


def rw_lora2(proj, col0, w2, a2, g2, w0, a0):
    bsz, s, _ = proj.shape
    d = w2.shape[1]
    tm = _pick(s, (256, 128))
    cb = col0 // RW_TN
    hspec = lambda k: pl.BlockSpec((1, tm, RW_TN), lambda b, i, k=k: (b, i, cb + k))
    wspec = pl.BlockSpec((RW_TN, d), lambda b, i: (0, 0))
    vspec = pl.BlockSpec((1, d), lambda b, i: (0, 0))
    ospec = pl.BlockSpec((1, tm, d), lambda b, i: (b, i, 0))
    oshape = jax.ShapeDtypeStruct((bsz, s, d), F32)
    return pl.pallas_call(
        _rw_lora2_kernel,
        grid=(bsz, s // tm),
        in_specs=[hspec(0), hspec(1), hspec(2), wspec, wspec, wspec, vspec, vspec],
        out_specs=[ospec, ospec, ospec],
        out_shape=[oshape, oshape, oshape],
        compiler_params=_cparams(("parallel", "parallel")),
        name="rw_lora2",
    )(proj, proj, proj, w2, a2, g2, w0.reshape(1, d), a0.reshape(1, d))


def _rw_scan_kernel(r_ref, k_ref, v_ref, wp_ref, ap_ref, kk_ref, ka_ref, rk_ref, lnw_ref, lnb_ref,
                    o_ref, st_ref, dec_ref, kkn_ref, bb_ref, kh_ref, *, tb, lw):
    hd = RW_HEADDIM
    bh = r_ref.shape[-1]

    @pl.when(pl.program_id(0) == 0)
    def _():
        st_ref[...] = jnp.zeros_like(st_ref)

    def prep(t, carry):
        a = _sigmoid(ap_ref[t])
        kraw = k_ref[t]
        kk = kraw * kk_ref[...]
        nrm = jnp.sum(kk * kk, axis=0, keepdims=True)
        kkn = kk * lax.rsqrt(jnp.maximum(nrm, 1e-24))
        dec_ref[t] = jnp.exp(-jnp.exp(-_softplus(-wp_ref[t]) - 0.5))
        kkn_ref[t] = kkn
        bb_ref[t] = kkn * a
        kh_ref[t] = kraw * (1.0 + (a - 1.0) * ka_ref[...])
        return carry

    lax.fori_loop(0, tb, prep, 0)

    sub = 8

    def rows8(ref, t, kb, lanes):
        return ref[t, pl.ds(pl.multiple_of(kb * sub, sub), sub), lanes]

    def bcast(tile, j):
        return jnp.broadcast_to(tile[j:j + 1, :], (hd, lw))

    def step(t, carry):
        for c in range(bh // lw):
            lanes = slice(c * lw, (c + 1) * lw)

            def pass1(kb, sa):
                a8 = rows8(kkn_ref, t, kb, lanes)
                for j in range(sub):
                    sa = sa + st_ref[kb * sub + j, :, lanes] * bcast(a8, j)
                return sa

            sa = -lax.fori_loop(0, hd // sub, pass1, jnp.zeros((hd, lw), F32))
            vt = v_ref[t, :, lanes]

            def pass2(kb, y):
                d8 = rows8(dec_ref, t, kb, lanes)
                b8 = rows8(bb_ref, t, kb, lanes)
                k8 = rows8(kh_ref, t, kb, lanes)
                r8 = rows8(r_ref, t, kb, lanes)
                for j in range(sub):
                    s_new = (st_ref[kb * sub + j, :, lanes] * bcast(d8, j)
                             + sa * bcast(b8, j) + vt * bcast(k8, j))
                    st_ref[kb * sub + j, :, lanes] = s_new
                    y = y + s_new * bcast(r8, j)
                return y

            y = lax.fori_loop(0, hd // sub, pass2, jnp.zeros((hd, lw), F32))
            mu = jnp.mean(y, axis=0, keepdims=True)
            yc = y - mu
            var = jnp.mean(yc * yc, axis=0, keepdims=True)
            yn = yc * lax.rsqrt(var + RW_GN_EPS) * lnw_ref[:, lanes] + lnb_ref[:, lanes]
            bonus = jnp.sum(r_ref[t, :, lanes] * kh_ref[t, :, lanes] * rk_ref[:, lanes],
                            axis=0, keepdims=True)
            o_ref[t, :, lanes] = yn + bonus * vt
        return carry

    lax.fori_loop(0, tb, step, 0)


def rw_scan(r, k, v, wpre, apre, k_k, k_a, r_k, ln_w, ln_b):
    s, hd, bh = r.shape
    tb = _pick(s, (16, 8))
    lw = min(LANE, bh)
    seq = pl.BlockSpec((tb, hd, bh), lambda i: (i, 0, 0))
    par = pl.BlockSpec((hd, bh), lambda i: (0, 0))
    return pl.pallas_call(
        functools.partial(_rw_scan_kernel, tb=tb, lw=lw),
        grid=(s // tb,),
        in_specs=[seq] * 5 + [par] * 5,
        out_specs=seq,
        out_shape=jax.ShapeDtypeStruct((s, hd, bh), F32),
        scratch_shapes=[pltpu.VMEM((hd, hd, bh), F32)] + [pltpu.VMEM((tb, hd, bh), F32)] * 4,
        compiler_params=_cparams(("arbitrary",)),
        name="rw_scan",
    )(r, k, v, wpre, apre, k_k, k_a, r_k, ln_w, ln_b)


def rwkv7_layer(x, mod, mu, w_rkv, w0, w1, w2, a0, a1, a2, g1, g2, k_k, k_a, r_k, gn_w, gn_b,
                w_out, ln_w, ln_b, alpha):
    bsz, s, d = x.shape
    nh = d // RW_HEADDIM
    pad_c = lambda w: _pad_cols(w, RW_TN)
    pad_r = lambda w: jnp.pad(w, ((0, RW_TN - w.shape[0]), (0, 0)))
    secs = [w_rkv[0], w_rkv[1], w_rkv[2], pad_c(w1), pad_c(a1), pad_c(g1)]
    w_all = jnp.concatenate(secs, axis=1).astype(BF16)
    rkv, lora = rw_mix_mm(x, mod, mu, w_all)
    wpre, apre, gate = rw_lora2(lora, 0, pad_r(w2).astype(BF16), pad_r(a2).astype(BF16),
                                pad_r(g2).astype(BF16), w0, a0)

    def to_scan(t):
        return t.reshape(bsz, s, nh, RW_HEADDIM).transpose(1, 3, 0, 2).reshape(s, RW_HEADDIM, bsz * nh)

    def par(p):
        return jnp.tile(p.reshape(nh, RW_HEADDIM).T[:, None, :], (1, bsz, 1)).reshape(RW_HEADDIM, bsz * nh)

    y = rw_scan(to_scan(rkv[0]), to_scan(rkv[1]), to_scan(rkv[2]), to_scan(wpre), to_scan(apre),
                par(k_k), par(k_a), par(r_k.reshape(-1)), par(gn_w), par(gn_b))
    y = y.reshape(s, RW_HEADDIM, bsz, nh).transpose(2, 0, 3, 1).reshape(bsz, s, d)
    return mm_res_ln(y, w_out.astype(BF16), x, mod, 2, ln_w, ln_b, alpha, h2=gate)


def kernel(x, c, ada_w, ada_b, ln_w, ln_b, hg_w_in, hg_lb_logits, hg_norm_w, hg_w_out, ssd_w_in, ssd_conv_w, ssd_conv_b, ssd_dt_bias, ssd_a_log, ssd_d, ssd_norm_w, ssd_w_out, da_w_in, da_w_out, rw_mu, rw_w_rkv, rw_w0, rw_w1, rw_w2, rw_a0, rw_a1, rw_a2, rw_g1, rw_g2, rw_k_k, rw_k_a, rw_r_k, rw_ln_w, rw_ln_b, rw_w_out, ffn_w_gu, ffn_w_down, moe_router, moe_w_gu, moe_w_down):
    depth = ada_w.shape[0]
    bsz, _, d = x.shape
    alpha = (2.0 * depth) ** 0.25
    lbs = lower_bounds(hg_lb_logits)
    mods = ada_modulation(c, ada_w, ada_b).reshape(depth, bsz, N_MOD, d)
    n_experts = moe_w_gu.shape[1]
    moe_gu = moe_w_gu.astype(BF16).reshape((-1,) + moe_w_gu.shape[2:])
    moe_dn = moe_w_down.astype(BF16).reshape((-1,) + moe_w_down.shape[2:])
    for i in range(depth):
        mixer, j = i % 4, i // 4
        mod = mods[i]
        if mixer == 0:
            x = hgrn2_layer(x, mod, hg_w_in[j], lbs[i], hg_norm_w[j], hg_w_out[j],
                            ln_w[i, 0], ln_b[i, 0], alpha)
        elif mixer == 1:
            x = mamba2_layer(x, mod, ssd_w_in[j], ssd_conv_w[j], ssd_conv_b[j], ssd_dt_bias[j],
                             ssd_a_log[j], ssd_d[j], ssd_norm_w[j], ssd_w_out[j],
                             ln_w[i, 0], ln_b[i, 0], alpha)
        elif mixer == 2:
            x = dilated_attention_layer(x, mod, da_w_in[j], da_w_out[j], ln_w[i, 0], ln_b[i, 0], alpha)
        else:
            x = rwkv7_layer(x, mod, rw_mu[j], rw_w_rkv[j], rw_w0[j], rw_w1[j], rw_w2[j], rw_a0[j],
                            rw_a1[j], rw_a2[j], rw_g1[j], rw_g2[j], rw_k_k[j], rw_k_a[j], rw_r_k[j],
                            rw_ln_w[j], rw_ln_b[j], rw_w_out[j], ln_w[i, 0], ln_b[i, 0], alpha)
        if i % 2 == 0:
            x = dense_ffn(x, mod, ffn_w_gu[i // 2], ffn_w_down[i // 2], ln_w[i, 1], ln_b[i, 1], alpha)
        else:
            x = moe_ffn(x, mod, moe_router[i // 2], moe_gu, moe_dn, (i // 2) * n_experts,
                        ln_w[i, 1], ln_b[i, 1], alpha)
    return x
```
